```python
import math
import jax, jax.numpy as jnp
from jax import lax
import numpy as np

D_MODEL = 1024
BATCH = 8
SEQ = 8192
DEPTH = 1

HEAD_DIM = 64
N_Q_HEADS = 8
N_KV_HEADS = 2
GQA_GROUP = N_Q_HEADS // N_KV_HEADS
D_ATTN = N_Q_HEADS * HEAD_DIM
D_KV = N_KV_HEADS * HEAD_DIM
N_CONV_GROUPS = 8
D_CONV = D_MODEL - D_ATTN
D_MIX = D_ATTN + D_CONV
D_IN = D_ATTN + 2 * D_KV + 3 * D_CONV
WINDOW = 128
BLOCK = 128
CONV_WIDTH = 3
D_FF = 2816
ROPE_THETA = 10000.0
EPS = 1e-6
N_MOD = 6

kernel_name = "hybrid_swa_shortconv_convffn_adaln"


def rms_norm(x, g):
    xf = x.astype(jnp.float32)
    y = xf * lax.rsqrt(jnp.mean(xf * xf, axis=-1, keepdims=True) + EPS)
    return (y * g.astype(jnp.float32)).astype(x.dtype)


def causal_dwconv3(x, w, b=None):
    s = x.shape[1]
    xp = jnp.pad(x, ((0, 0), (CONV_WIDTH - 1, 0), (0, 0)))
    y = xp[:, 0:s] * w[0] + xp[:, 1:s + 1] * w[1] + xp[:, 2:s + 2] * w[2]
    if b is not None:
        y = y + b
    return y


def rope(x, positions):
    half = HEAD_DIM // 2
    inv_freq = ROPE_THETA ** (-jnp.arange(0, half, dtype=jnp.float32) / half)
    ang = positions.astype(jnp.float32)[..., None] * inv_freq
    cos = jnp.cos(ang)[:, :, None, :]
    sin = jnp.sin(ang)[:, :, None, :]
    xf = x.astype(jnp.float32)
    x1, x2 = xf[..., :half], xf[..., half:]
    out = jnp.concatenate([x1 * cos - x2 * sin, x2 * cos + x1 * sin], axis=-1)
    return out.astype(x.dtype)


def sliding_window_attention(q, k, v, sinks):
    bsz, s = q.shape[0], q.shape[1]
    nb = s // BLOCK
    qb = q.reshape(bsz, nb, BLOCK, N_KV_HEADS, GQA_GROUP, HEAD_DIM)
    kb = k.reshape(bsz, nb, BLOCK, N_KV_HEADS, HEAD_DIM)
    vb = v.reshape(bsz, nb, BLOCK, N_KV_HEADS, HEAD_DIM)
    pad = ((0, 0), (1, 0), (0, 0), (0, 0), (0, 0))
    keys = jnp.concatenate([jnp.pad(kb, pad)[:, :-1], kb], axis=2)
    vals = jnp.concatenate([jnp.pad(vb, pad)[:, :-1], vb], axis=2)
    scores = jnp.einsum('bnqhgd,bnkhd->bnhgqk', qb, keys).astype(jnp.float32)
    scores = scores * (1.0 / math.sqrt(HEAD_DIM))
    qi = jnp.arange(BLOCK)[:, None]
    kj = jnp.arange(2 * BLOCK)[None, :]
    diff = qi + BLOCK - kj
    band = (diff >= 0) & (diff < WINDOW)
    blk = jnp.arange(nb)[:, None, None]
    valid = band[None] & ((blk > 0) | (kj[None] >= BLOCK))
    scores = jnp.where(valid[None, :, None, None], scores, -1e30)
    sink = sinks.astype(jnp.float32).reshape(1, 1, N_KV_HEADS, GQA_GROUP, 1, 1)
    sink = jnp.broadcast_to(sink, scores.shape[:-1] + (1,))
    probs = jax.nn.softmax(jnp.concatenate([scores, sink], axis=-1), axis=-1)[..., :-1]
    out = jnp.einsum('bnhgqk,bnkhd->bnqhgd', probs.astype(v.dtype), vals)
    return out.reshape(bsz, s, D_ATTN)


def setup_inputs(seed: int = 0) -> dict:
    key = jax.random.key(seed)
    ks = jax.random.split(key, 20)
    f32 = jnp.float32
    nrm = lambda k, shape, scale: (jax.random.normal(k, shape, f32) * scale)
    gain = lambda k, shape: 1.0 + 0.05 * jax.random.normal(k, shape, f32)
    L = DEPTH
    return {
        "x": jax.random.normal(ks[0], (BATCH, SEQ, D_MODEL), f32),
        "c": jax.random.normal(ks[1], (BATCH, D_MODEL), f32),
        "positions": jnp.arange(SEQ, dtype=jnp.int32)[None, :]
                     + jax.random.randint(ks[2], (BATCH, 1), 0, 4096, dtype=jnp.int32),
        "ada_w": nrm(ks[3], (L, D_MODEL, N_MOD * D_MODEL), 0.5 * D_MODEL ** -0.5),
        "ada_b": nrm(ks[4], (L, N_MOD * D_MODEL), 0.02),
        "norm1_g": gain(ks[5], (L, D_MODEL)),
        "w_in": nrm(ks[6], (L, D_MODEL, D_IN), D_MODEL ** -0.5),
        "b_in": nrm(ks[7], (L, D_IN), 0.02),
        "conv_w": nrm(ks[8], (L, CONV_WIDTH, D_CONV), CONV_WIDTH ** -0.5),
        "attn_sinks": nrm(ks[9], (L, N_Q_HEADS), 0.5),
        "out_norm_attn_g": gain(ks[10], (L, D_ATTN)),
        "out_norm_conv_g": gain(ks[11], (L, D_CONV)),
        "w_o": nrm(ks[12], (L, D_MIX, D_MODEL), D_MIX ** -0.5),
        "norm2_g": gain(ks[13], (L, D_MODEL)),
        "w_up": nrm(ks[14], (L, D_MODEL, 2 * D_FF), D_MODEL ** -0.5),
        "ffn_conv_w": nrm(ks[15], (L, CONV_WIDTH, D_FF), CONV_WIDTH ** -0.5),
        "ffn_conv_b": nrm(ks[16], (L, D_FF), 0.02),
        "w_down": nrm(ks[17], (L, D_FF, D_MODEL), D_FF ** -0.5),
        "final_norm_g": gain(ks[18], (D_MODEL,)),
    }


def reference(x, c, positions, ada_w, ada_b, norm1_g, w_in, b_in, conv_w, attn_sinks,
              out_norm_attn_g, out_norm_conv_g, w_o, norm2_g, w_up, ffn_conv_w,
              ffn_conv_b, w_down, final_norm_g):
    bsz, s, _ = x.shape
    c_act = jax.nn.silu(c)
    for l in range(DEPTH):
        mod = c_act @ ada_w[l] + ada_b[l]
        sh1, sc1, g1, sh2, sc2, g2 = [m[:, None, :] for m in jnp.split(mod, N_MOD, axis=-1)]

        h = rms_norm(x, norm1_g[l]) * (1.0 + sc1) + sh1
        proj = h @ w_in[l] + b_in[l]
        q, k, v, gb, gc, xs = jnp.split(
            proj, np.cumsum([D_ATTN, D_KV, D_KV, D_CONV, D_CONV]).tolist(), axis=-1)
        q = rope(q.reshape(bsz, s, N_Q_HEADS, HEAD_DIM), positions)
        k = rope(k.reshape(bsz, s, N_KV_HEADS, HEAD_DIM), positions)
        v = v.reshape(bsz, s, N_KV_HEADS, HEAD_DIM)
        y_attn = sliding_window_attention(q, k, v, attn_sinks[l])
        y_conv = gc * causal_dwconv3(gb * xs, conv_w[l])
        y = jnp.concatenate([rms_norm(y_attn, out_norm_attn_g[l]),
                             rms_norm(y_conv, out_norm_conv_g[l])], axis=-1)
        x = x + g1 * (y @ w_o[l])

        h = rms_norm(x, norm2_g[l]) * (1.0 + sc2) + sh2
        u = h @ w_up[l]
        gate, val = u[..., :D_FF], u[..., D_FF:]
        f = jax.nn.silu(causal_dwconv3(gate, ffn_conv_w[l], ffn_conv_b[l])) * val
        x = x + g2 * (f @ w_down[l])
    return rms_norm(x, final_norm_g)
```

```python
import functools
import math

import jax
import jax.numpy as jnp
from jax import lax
from jax.experimental import pallas as pl
from jax.experimental.pallas import tpu as pltpu

D_MODEL = 1024
HEAD_DIM = 64
N_Q_HEADS = 8
N_KV_HEADS = 2
D_ATTN = N_Q_HEADS * HEAD_DIM
D_KV = N_KV_HEADS * HEAD_DIM
D_CONV = D_MODEL - D_ATTN
D_QKV = D_ATTN + 2 * D_KV
D_IN = D_QKV + 3 * D_CONV
WINDOW = 128
D_FF = 2816
N_MOD = 6
ROPE_THETA = 10000.0
EPS = 1e-6
MASK_VALUE = -1e30

V7X_LANES = 128
V7X_SUBLANES = 8
V7X_MXU_COLS = 256
V7X_VMEM_BYTES = 64 * 1024 * 1024

SEQ_TILE = 512
FF_CHUNK = V7X_MXU_COLS
PAIR = 2 * HEAD_DIM

F32 = jnp.float32
BF16 = jnp.bfloat16


def _vmem_limit(resident_bytes):
    return min(2 * resident_bytes, V7X_VMEM_BYTES - 4 * 1024 * 1024)


def _rms_norm(x, g):
    ms = jnp.mean(x * x, axis=-1, keepdims=True)
    return x * lax.rsqrt(ms + EPS) * g


def _causal_conv3(cur, prev, w):
    rows = lax.broadcasted_iota(jnp.int32, cur.shape, 0)
    p1 = prev[V7X_SUBLANES - 1:V7X_SUBLANES]
    p2 = prev[V7X_SUBLANES - 2:V7X_SUBLANES - 1]
    r1 = jnp.where(rows == 0, p1, pltpu.roll(cur, 1, 0))
    r2 = jnp.where(rows == 0, p2, jnp.where(rows == 1, p1, pltpu.roll(cur, 2, 0)))
    return r2 * w[0:1] + r1 * w[1:2] + cur * w[2:3]


def _mod_kernel(c_ref, w_ref, b_ref, o_ref):
    c_act = jax.nn.silu(c_ref[...])
    o_ref[...] = jnp.dot(c_act.astype(BF16), w_ref[...].astype(BF16),
                         preferred_element_type=F32) + b_ref[...]


def _modulation(c, ada_w, ada_b):
    bsz = c.shape[0]
    n = ada_w.shape[1]
    blk = D_MODEL
    return pl.pallas_call(
        _mod_kernel,
        grid=(n // blk,),
        in_specs=[
            pl.BlockSpec((bsz, D_MODEL), lambda j: (0, 0)),
            pl.BlockSpec((D_MODEL, blk), lambda j: (0, j)),
            pl.BlockSpec((1, blk), lambda j: (0, j)),
        ],
        out_specs=pl.BlockSpec((bsz, blk), lambda j: (0, j)),
        out_shape=jax.ShapeDtypeStruct((bsz, n), F32),
        name="adaln_modulation",
    )(c, ada_w, ada_b.reshape(1, n))


def _mixer_kernel(sinks_ref, x_ref, pos_ref, mod_ref, invf_ref, n1g_ref, win_ref, bin_ref,
                  cw_ref, ga_ref, gc_ref, wo_ref, o_ref,
                  q_s, k_s, v_s, ya_s, cc_s):
    t = pl.program_id(1)
    tile = x_ref.shape[1]
    nblk = tile // WINDOW

    @pl.when(t == 0)
    def _():
        k_s[:, 0:WINDOW, :] = jnp.zeros((N_KV_HEADS, WINDOW, PAIR), BF16)
        v_s[:, 0:WINDOW, :] = jnp.zeros((2 * N_KV_HEADS, WINDOW, PAIR), BF16)
        cc_s[...] = jnp.zeros(cc_s.shape, F32)

    x = x_ref[0]
    mod = mod_ref[0]
    h = _rms_norm(x, n1g_ref[...]) * (1.0 + mod[1:2]) + mod[0:1]
    hb = h.astype(BF16)

    qkv = jnp.dot(hb, win_ref[:, 0:D_QKV], preferred_element_type=F32) + bin_ref[:, 0:D_QKV]

    ang = invf_ref[...] * pos_ref[0].astype(F32)
    cos_f, sin_f = jnp.cos(ang), jnp.sin(ang)
    cos_t = jnp.concatenate([cos_f, cos_f, cos_f, cos_f], axis=0).T
    sin_t = jnp.concatenate([-sin_f, sin_f, -sin_f, sin_f], axis=0).T

    lane = lax.broadcasted_iota(jnp.int32, (1, PAIR), 1)
    low_head = lane < HEAD_DIM
    first_half = (lane % HEAD_DIM) < (HEAD_DIM // 2)

    def rope(u):
        rot = jnp.where(first_half, pltpu.roll(u, PAIR - HEAD_DIM // 2, 1),
                        pltpu.roll(u, HEAD_DIM // 2, 1))
        return u * cos_t + rot * sin_t

    scale = 1.0 / math.sqrt(HEAD_DIM)
    for p in range(D_ATTN // PAIR):
        q_s[:, p * PAIR:(p + 1) * PAIR] = (rope(qkv[:, p * PAIR:(p + 1) * PAIR]) * scale).astype(BF16)

    k = rope(qkv[:, D_ATTN:D_ATTN + D_KV])
    k_sw = pltpu.roll(k, HEAD_DIM, 1)
    v = qkv[:, D_ATTN + D_KV:D_QKV]
    v_sw = pltpu.roll(v, HEAD_DIM, 1)
    cur = pl.ds(WINDOW, tile)
    k_s[0, cur, :] = jnp.where(low_head, k, k_sw).astype(BF16)
    k_s[1, cur, :] = jnp.where(low_head, k_sw, k).astype(BF16)
    v_s[0, cur, :] = jnp.where(low_head, v, 0.0).astype(BF16)
    v_s[1, cur, :] = jnp.where(low_head, 0.0, v_sw).astype(BF16)
    v_s[2, cur, :] = jnp.where(low_head, v_sw, 0.0).astype(BF16)
    v_s[3, cur, :] = jnp.where(low_head, 0.0, v).astype(BF16)

    qi = lax.broadcasted_iota(jnp.int32, (2 * WINDOW, 2 * WINDOW), 0) % WINDOW
    kj = lax.broadcasted_iota(jnp.int32, (2 * WINDOW, 2 * WINDOW), 1)
    diff = qi + WINDOW - kj
    band = (diff >= 0) & (diff < WINDOW)
    band_first = band & ((kj >= WINDOW) | (t > 0))
    upper_rows = lax.broadcasted_iota(jnp.int32, (2 * WINDOW, 1), 0) >= WINDOW

    for i in range(nblk):
        mask = band_first if i == 0 else band
        qrows = pl.ds(i * WINDOW, WINDOW)
        krows = pl.ds(i * WINDOW, 2 * WINDOW)
        for p in range(D_ATTN // PAIR):
            g = p // (N_Q_HEADS // N_KV_HEADS // 2)
            qp = q_s[qrows, p * PAIR:(p + 1) * PAIR]
            zero = jnp.zeros_like(qp)
            lhs = jnp.concatenate([jnp.where(low_head, qp, zero), jnp.where(low_head, zero, qp)], axis=0)
            s = lax.dot_general(lhs, k_s[g, krows, :], (((1,), (1,)), ((), ())),
                                preferred_element_type=F32)
            s = jnp.where(mask, s, MASK_VALUE)
            sink = jnp.where(upper_rows, sinks_ref[2 * p + 1], sinks_ref[2 * p])
            m = jnp.maximum(jnp.max(s, axis=-1, keepdims=True), sink)
            e = jnp.exp(s - m)
            den = jnp.sum(e, axis=-1, keepdims=True) + jnp.exp(sink - m)
            eb = e.astype(BF16)
            o = (jnp.dot(eb[0:WINDOW], v_s[2 * g, krows, :], preferred_element_type=F32)
                 + jnp.dot(eb[WINDOW:], v_s[2 * g + 1, krows, :], preferred_element_type=F32))
            ya_s[qrows, p * PAIR:(p + 1) * PAIR] = o / jnp.where(low_head, den[0:WINDOW], den[WINDOW:])

    k_s[:, 0:WINDOW, :] = k_s[:, tile:tile + WINDOW, :]
    v_s[:, 0:WINDOW, :] = v_s[:, tile:tile + WINDOW, :]

    cproj = jnp.dot(hb, win_ref[:, D_QKV:D_IN], preferred_element_type=F32) + bin_ref[:, D_QKV:D_IN]
    bx = cproj[:, 0:D_CONV] * cproj[:, 2 * D_CONV:3 * D_CONV]
    y_conv = cproj[:, D_CONV:2 * D_CONV] * _causal_conv3(bx, cc_s[...], cw_ref[...])
    cc_s[...] = bx[tile - V7X_SUBLANES:tile]

    y = jnp.concatenate([_rms_norm(ya_s[...], ga_ref[...]), _rms_norm(y_conv, gc_ref[...])], axis=-1)
    o_ref[0] = x + mod[2:3] * jnp.dot(y.astype(BF16), wo_ref[...], preferred_element_type=F32)


def _token_mixer(x, positions, mod, norm1_g, w_in, b_in, conv_w, sinks, ga, gc, w_o):
    bsz, seq, _ = x.shape
    tile = SEQ_TILE
    assert seq % tile == 0 and tile % WINDOW == 0
    inv_freq = ROPE_THETA ** (-jnp.arange(0, HEAD_DIM // 2, dtype=F32) / (HEAD_DIM // 2))
    const = lambda shape: pl.BlockSpec(shape, lambda b, t: (0,) * len(shape),
                                       pipeline_mode=pl.Buffered(1))
    resident = (2 * 2 * tile * D_MODEL * 4 + (D_MODEL * D_IN + D_MODEL * D_MODEL) * 2
                + tile * D_IN * 4 + tile * (D_ATTN * 6 + 6 * PAIR * 2))
    return pl.pallas_call(
        _mixer_kernel,
        grid=(bsz, seq // tile),
        in_specs=[
            pl.BlockSpec(memory_space=pltpu.SMEM),
            pl.BlockSpec((1, tile, D_MODEL), lambda b, t: (b, t, 0)),
            pl.BlockSpec((1, 1, tile), lambda b, t: (b, 0, t)),
            pl.BlockSpec((1, N_MOD, D_MODEL), lambda b, t: (b, 0, 0)),
            const((HEAD_DIM // 2, 1)),
            const((1, D_MODEL)),
            const((D_MODEL, D_IN)),
            const((1, D_IN)),
            const((3, D_CONV)),
            const((1, D_ATTN)),
            const((1, D_CONV)),
            const((D_MODEL, D_MODEL)),
        ],
        out_specs=pl.BlockSpec((1, tile, D_MODEL), lambda b, t: (b, t, 0)),
        out_shape=jax.ShapeDtypeStruct(x.shape, F32),
        scratch_shapes=[
            pltpu.VMEM((tile, D_ATTN), BF16),
            pltpu.VMEM((N_KV_HEADS, tile + WINDOW, PAIR), BF16),
            pltpu.VMEM((2 * N_KV_HEADS, tile + WINDOW, PAIR), BF16),
            pltpu.VMEM((tile, D_ATTN), F32),
            pltpu.VMEM((V7X_SUBLANES, D_CONV), F32),
        ],
        compiler_params=pltpu.CompilerParams(
            dimension_semantics=("arbitrary", "arbitrary"),
            vmem_limit_bytes=_vmem_limit(resident)),
        name="token_mixer",
    )(sinks, x, positions.reshape(bsz, 1, seq), mod, inv_freq.reshape(-1, 1),
      norm1_g.reshape(1, -1), w_in, b_in.reshape(1, -1), conv_w, ga.reshape(1, -1),
      gc.reshape(1, -1), w_o)


def _ffn_kernel(x_ref, mod_ref, n2g_ref, wup_ref, cw_ref, cb_ref, wdn_ref, fg_ref, o_ref,
                f_s, gc_s):
    t = pl.program_id(1)
    tile = x_ref.shape[1]

    @pl.when(t == 0)
    def _():
        gc_s[...] = jnp.zeros(gc_s.shape, F32)

    x = x_ref[0]
    mod = mod_ref[0]
    h = _rms_norm(x, n2g_ref[...]) * (1.0 + mod[4:5]) + mod[3:4]
    hb = h.astype(BF16)

    for c in range(D_FF // FF_CHUNK):
        cols = pl.ds(c * FF_CHUNK, FF_CHUNK)
        gate = jnp.dot(hb, wup_ref[:, cols], preferred_element_type=F32)
        val = jnp.dot(hb, wup_ref[:, pl.ds(D_FF + c * FF_CHUNK, FF_CHUNK)], preferred_element_type=F32)
        conv = _causal_conv3(gate, gc_s[:, cols], cw_ref[:, cols]) + cb_ref[:, cols]
        gc_s[:, cols] = gate[tile - V7X_SUBLANES:tile]
        f_s[:, cols] = (jax.nn.silu(conv) * val).astype(BF16)

    x2 = x + mod[5:6] * jnp.dot(f_s[...], wdn_ref[...], preferred_element_type=F32)
    o_ref[0] = _rms_norm(x2, fg_ref[...])


def _channel_mixer(x, mod, norm2_g, w_up, conv_w, conv_b, w_down, final_g):
    bsz, seq, _ = x.shape
    tile = SEQ_TILE
    assert seq % tile == 0 and D_FF % FF_CHUNK == 0
    const = lambda shape: pl.BlockSpec(shape, lambda b, t: (0,) * len(shape),
                                       pipeline_mode=pl.Buffered(1))
    resident = (2 * 2 * tile * D_MODEL * 4 + 3 * D_MODEL * D_FF * 2 + tile * D_FF * 2
                + 4 * tile * FF_CHUNK * 4)
    return pl.pallas_call(
        _ffn_kernel,
        grid=(bsz, seq // tile),
        in_specs=[
            pl.BlockSpec((1, tile, D_MODEL), lambda b, t: (b, t, 0)),
            pl.BlockSpec((1, N_MOD, D_MODEL), lambda b, t: (b, 0, 0)),
            const((1, D_MODEL)),
            const((D_MODEL, 2 * D_FF)),
            const((3, D_FF)),
            const((1, D_FF)),
            const((D_FF, D_MODEL)),
            const((1, D_MODEL)),
        ],
        out_specs=pl.BlockSpec((1, tile, D_MODEL), lambda b, t: (b, t, 0)),
        out_shape=jax.ShapeDtypeStruct(x.shape, F32),
        scratch_shapes=[
            pltpu.VMEM((tile, D_FF), BF16),
            pltpu.VMEM((V7X_SUBLANES, D_FF), F32),
        ],
        compiler_params=pltpu.CompilerParams(
            dimension_semantics=("arbitrary", "arbitrary"),
            vmem_limit_bytes=_vmem_limit(resident)),
        name="channel_mixer",
    )(x, mod, norm2_g.reshape(1, -1), w_up, conv_w, conv_b.reshape(1, -1), w_down,
      final_g.reshape(1, -1))


def kernel(x, c, positions, ada_w, ada_b, norm1_g, w_in, b_in, conv_w, attn_sinks, out_norm_attn_g, out_norm_conv_g, w_o, norm2_g, w_up, ffn_conv_w, ffn_conv_b, w_down, final_norm_g):
    depth = ada_w.shape[0]
    bsz = x.shape[0]
    for l in range(depth):
        mod = _modulation(c, ada_w[l], ada_b[l]).reshape(bsz, N_MOD, D_MODEL)
        x = _token_mixer(x, positions, mod, norm1_g[l], w_in[l].astype(BF16), b_in[l], conv_w[l],
                         attn_sinks[l], out_norm_attn_g[l], out_norm_conv_g[l], w_o[l].astype(BF16))
        last = l == depth - 1
        assert last, "the final RMSNorm is fused into the last layer's channel mixer"
        x = _channel_mixer(x, mod, norm2_g[l], w_up[l].astype(BF16), ffn_conv_w[l], ffn_conv_b[l],
                           w_down[l].astype(BF16), final_norm_g)
    return x
```

```python
import functools
import math

import jax
import jax.numpy as jnp
from jax import lax
from jax.experimental import pallas as pl
from jax.experimental.pallas import tpu as pltpu

D_MODEL = 1024
HEAD_DIM = 64
N_Q_HEADS = 8
N_KV_HEADS = 2
D_ATTN = N_Q_HEADS * HEAD_DIM
D_KV = N_KV_HEADS * HEAD_DIM
D_CONV = D_MODEL - D_ATTN
D_QKV = D_ATTN + 2 * D_KV
D_IN = D_QKV + 3 * D_CONV
WINDOW = 128
D_FF = 2816
N_MOD = 6
ROPE_THETA = 10000.0
EPS = 1e-6
MASK_VALUE = -1e30

V7X_LANES = 128
V7X_SUBLANES = 8
V7X_MXU_COLS = 256
V7X_VMEM_BYTES = 64 * 1024 * 1024

SEQ_TILE = 512
FF_CHUNK = V7X_MXU_COLS
PAIR = 2 * HEAD_DIM

F32 = jnp.float32
BF16 = jnp.bfloat16


def _vmem_limit(resident_bytes):
    return min(2 * resident_bytes, V7X_VMEM_BYTES - 4 * 1024 * 1024)


def _rms_norm(x, g):
    ms = jnp.mean(x * x, axis=-1, keepdims=True)
    return x * lax.rsqrt(ms + EPS) * g


def _causal_conv3(cur, prev, w):
    rows = lax.broadcasted_iota(jnp.int32, cur.shape, 0)
    p1 = prev[V7X_SUBLANES - 1:V7X_SUBLANES]
    p2 = prev[V7X_SUBLANES - 2:V7X_SUBLANES - 1]
    r1 = jnp.where(rows == 0, p1, pltpu.roll(cur, 1, 0))
    r2 = jnp.where(rows == 0, p2, jnp.where(rows == 1, p1, pltpu.roll(cur, 2, 0)))
    return r2 * w[0:1] + r1 * w[1:2] + cur * w[2:3]


def _mod_kernel(c_ref, w_ref, b_ref, o_ref):
    c_act = jax.nn.silu(c_ref[...])
    o_ref[...] = jnp.dot(c_act.astype(BF16), w_ref[...].astype(BF16),
                         preferred_element_type=F32) + b_ref[...]


def _modulation(c, ada_w, ada_b):
    bsz = c.shape[0]
    n = ada_w.shape[1]
    blk = D_MODEL
    return pl.pallas_call(
        _mod_kernel,
        grid=(n // blk,),
        in_specs=[
            pl.BlockSpec((bsz, D_MODEL), lambda j: (0, 0)),
            pl.BlockSpec((D_MODEL, blk), lambda j: (0, j)),
            pl.BlockSpec((1, blk), lambda j: (0, j)),
        ],
        out_specs=pl.BlockSpec((bsz, blk), lambda j: (0, j)),
        out_shape=jax.ShapeDtypeStruct((bsz, n), F32),
        name="adaln_modulation",
    )(c, ada_w, ada_b.reshape(1, n))


def _mixer_kernel(sinks_ref, x_ref, pos_ref, mod_ref, invf_ref, n1g_ref, win_ref, bin_ref,
                  cw_ref, ga_ref, gc_ref, wo_ref, o_ref,
                  q_s, k_s, v_s, ya_s, cc_s, s_s, m_s, p_s, t_s, y_s):
    t = pl.program_id(1)
    tile = x_ref.shape[1]
    nblk = tile // WINDOW

    lane = lax.broadcasted_iota(jnp.int32, (1, PAIR), 1)
    low_head = lane < HEAD_DIM
    first_half = (lane % HEAD_DIM) < (HEAD_DIM // 2)

    @pl.when(t > 0)
    def _():
        k_s[:, 0:WINDOW, :] = k_s[:, tile:tile + WINDOW, :]
        v_s[:, 0:WINDOW, 0:PAIR] = v_s[:, tile:tile + WINDOW, 0:PAIR]

    @pl.when(t == 0)
    def _():
        k_s[:, 0:WINDOW, :] = jnp.zeros((N_KV_HEADS, WINDOW, PAIR), BF16)
        v_s[:, 0:WINDOW, 0:PAIR] = jnp.zeros((2 * N_KV_HEADS, WINDOW, PAIR), BF16)
        ones_low = jnp.broadcast_to(jnp.where(low_head, 1.0, 0.0).astype(BF16), (tile + WINDOW, PAIR))
        ones_high = jnp.broadcast_to(jnp.where(low_head, 0.0, 1.0).astype(BF16), (tile + WINDOW, PAIR))
        for g in range(N_KV_HEADS):
            v_s[2 * g, :, PAIR:] = ones_low
            v_s[2 * g + 1, :, PAIR:] = ones_high
        cc_s[...] = jnp.zeros(cc_s.shape, F32)

    x = x_ref[0]
    mod = mod_ref[0]
    h = _rms_norm(x, n1g_ref[...]) * (1.0 + mod[1:2]) + mod[0:1]
    hb = h.astype(BF16)

    qkv = jnp.dot(hb, win_ref[:, 0:D_QKV], preferred_element_type=F32) + bin_ref[:, 0:D_QKV]

    ang = invf_ref[...] * pos_ref[0].astype(F32)
    cos_f, sin_f = jnp.cos(ang), jnp.sin(ang)
    cos_t = jnp.concatenate([cos_f, cos_f, cos_f, cos_f], axis=0).T
    sin_t = jnp.concatenate([-sin_f, sin_f, -sin_f, sin_f], axis=0).T

    def rope(u):
        rot = jnp.where(first_half, pltpu.roll(u, PAIR - HEAD_DIM // 2, 1),
                        pltpu.roll(u, HEAD_DIM // 2, 1))
        return u * cos_t + rot * sin_t

    scale = 1.0 / math.sqrt(HEAD_DIM)
    for p in range(D_ATTN // PAIR):
        q_s[:, p * PAIR:(p + 1) * PAIR] = (rope(qkv[:, p * PAIR:(p + 1) * PAIR]) * scale).astype(BF16)

    k = rope(qkv[:, D_ATTN:D_ATTN + D_KV])
    k_sw = pltpu.roll(k, HEAD_DIM, 1)
    v = qkv[:, D_ATTN + D_KV:D_QKV]
    v_sw = pltpu.roll(v, HEAD_DIM, 1)
    cur = pl.ds(WINDOW, tile)
    k_s[0, cur, :] = jnp.where(low_head, k, k_sw).astype(BF16)
    k_s[1, cur, :] = jnp.where(low_head, k_sw, k).astype(BF16)
    v_s[0, cur, 0:PAIR] = jnp.where(low_head, v, 0.0).astype(BF16)
    v_s[1, cur, 0:PAIR] = jnp.where(low_head, 0.0, v_sw).astype(BF16)
    v_s[2, cur, 0:PAIR] = jnp.where(low_head, v_sw, 0.0).astype(BF16)
    v_s[3, cur, 0:PAIR] = jnp.where(low_head, 0.0, v).astype(BF16)

    cproj = jnp.dot(hb, win_ref[:, D_QKV:D_IN], preferred_element_type=F32) + bin_ref[:, D_QKV:D_IN]
    bx = cproj[:, 0:D_CONV] * cproj[:, 2 * D_CONV:3 * D_CONV]
    y_conv = cproj[:, D_CONV:2 * D_CONV] * _causal_conv3(bx, cc_s[...], cw_ref[...])
    cc_s[...] = bx[tile - V7X_SUBLANES:tile]
    y_s[:, D_ATTN:] = _rms_norm(y_conv, gc_ref[...]).astype(BF16)

    qi = lax.broadcasted_iota(jnp.int32, (WINDOW, 2 * WINDOW), 0)
    kj = lax.broadcasted_iota(jnp.int32, (WINDOW, 2 * WINDOW), 1)
    diff = qi + WINDOW - kj
    band = (diff >= 0) & (diff < WINDOW)
    band_first = band & ((kj >= WINDOW) | (t > 0))
    group = N_Q_HEADS // N_KV_HEADS

    units = [(i, g) for i in range(nblk) for g in range(N_KV_HEADS)]

    for u, (i, g) in enumerate(units):
        mask = band_first if i == 0 else band
        pieces = []
        for p in range(g * group // 2, (g + 1) * group // 2):
            qp = q_s[pl.ds(i * WINDOW, WINDOW), p * PAIR:(p + 1) * PAIR]
            zero = jnp.zeros_like(qp)
            pieces += [jnp.where(low_head, qp, zero), jnp.where(low_head, zero, qp)]
        s = lax.dot_general(jnp.concatenate(pieces, axis=0), k_s[g, pl.ds(i * WINDOW, 2 * WINDOW), :],
                            (((1,), (1,)), ((), ())), preferred_element_type=F32)
        for j in range(group):
            sj = jnp.where(mask, s[j * WINDOW:(j + 1) * WINDOW], MASK_VALUE)
            s_s[u * group + j] = sj
            m = jnp.maximum(jnp.max(sj, axis=-1, keepdims=True), sinks_ref[g * group + j])
            m_s[u * group + j] = jnp.broadcast_to(m, (WINDOW, PAIR))

    for u, (i, g) in enumerate(units):
        for j in range(group):
            m = m_s[u * group + j]
            e = jnp.exp(s_s[u * group + j] - jnp.concatenate([m, m], axis=1))
            p_s[u, pl.ds((j // 2) * WINDOW, WINDOW), pl.ds((j % 2) * 2 * WINDOW, 2 * WINDOW)] = e.astype(BF16)
        for jj in range(group // 2):
            sink_lo = jnp.exp(sinks_ref[g * group + 2 * jj] - m_s[u * group + 2 * jj])
            sink_hi = jnp.exp(sinks_ref[g * group + 2 * jj + 1] - m_s[u * group + 2 * jj + 1])
            t_s[u * (group // 2) + jj] = jnp.where(low_head, sink_lo, sink_hi)

    for u, (i, g) in enumerate(units):
        krows = pl.ds(i * WINDOW, 2 * WINDOW)
        rhs = jnp.concatenate([v_s[2 * g, krows, :], v_s[2 * g + 1, krows, :]], axis=0)
        o = jnp.dot(p_s[u], rhs, preferred_element_type=F32)
        for jj in range(group // 2):
            oj = o[jj * WINDOW:(jj + 1) * WINDOW]
            den = oj[:, PAIR:] + t_s[u * (group // 2) + jj]
            p = g * group // 2 + jj
            ya_s[pl.ds(i * WINDOW, WINDOW), p * PAIR:(p + 1) * PAIR] = oj[:, 0:PAIR] / den

    y_s[:, 0:D_ATTN] = _rms_norm(ya_s[...], ga_ref[...]).astype(BF16)
    o_ref[0] = x + mod[2:3] * jnp.dot(y_s[...], wo_ref[...], preferred_element_type=F32)


def _token_mixer(x, positions, mod, norm1_g, w_in, b_in, conv_w, sinks, ga, gc, w_o):
    bsz, seq, _ = x.shape
    tile = SEQ_TILE
    assert seq % tile == 0 and tile % WINDOW == 0
    inv_freq = ROPE_THETA ** (-jnp.arange(0, HEAD_DIM // 2, dtype=F32) / (HEAD_DIM // 2))
    const = lambda shape: pl.BlockSpec(shape, lambda b, t: (0,) * len(shape),
                                       pipeline_mode=pl.Buffered(1))
    nhb = (tile // WINDOW) * N_Q_HEADS
    resident = (2 * 2 * tile * D_MODEL * 4 + (D_MODEL * D_IN + D_MODEL * D_MODEL) * 2
                + tile * D_IN * 4 + tile * (D_ATTN * 6 + 6 * PAIR * 2)
                + nhb * WINDOW * WINDOW * (8 + 4 + 2 + 2))
    return pl.pallas_call(
        _mixer_kernel,
        grid=(bsz, seq // tile),
        in_specs=[
            pl.BlockSpec(memory_space=pltpu.SMEM),
            pl.BlockSpec((1, tile, D_MODEL), lambda b, t: (b, t, 0)),
            pl.BlockSpec((1, 1, tile), lambda b, t: (b, 0, t)),
            pl.BlockSpec((1, N_MOD, D_MODEL), lambda b, t: (b, 0, 0)),
            const((HEAD_DIM // 2, 1)),
            const((1, D_MODEL)),
            const((D_MODEL, D_IN)),
            const((1, D_IN)),
            const((3, D_CONV)),
            const((1, D_ATTN)),
            const((1, D_CONV)),
            const((D_MODEL, D_MODEL)),
        ],
        out_specs=pl.BlockSpec((1, tile, D_MODEL), lambda b, t: (b, t, 0)),
        out_shape=jax.ShapeDtypeStruct(x.shape, F32),
        scratch_shapes=[
            pltpu.VMEM((tile, D_ATTN), BF16),
            pltpu.VMEM((N_KV_HEADS, tile + WINDOW, PAIR), BF16),
            pltpu.VMEM((2 * N_KV_HEADS, tile + WINDOW, 2 * PAIR), BF16),
            pltpu.VMEM((tile, D_ATTN), F32),
            pltpu.VMEM((V7X_SUBLANES, D_CONV), F32),
            pltpu.VMEM((nhb, WINDOW, 2 * WINDOW), F32),
            pltpu.VMEM((nhb, WINDOW, PAIR), F32),
            pltpu.VMEM((nhb // 4, 2 * WINDOW, 4 * WINDOW), BF16),
            pltpu.VMEM((nhb // 2, WINDOW, PAIR), F32),
            pltpu.VMEM((tile, D_MODEL), BF16),
        ],
        compiler_params=pltpu.CompilerParams(
            dimension_semantics=("arbitrary", "arbitrary"),
            vmem_limit_bytes=_vmem_limit(resident)),
        name="token_mixer",
    )(sinks, x, positions.reshape(bsz, 1, seq), mod, inv_freq.reshape(-1, 1),
      norm1_g.reshape(1, -1), w_in, b_in.reshape(1, -1), conv_w, ga.reshape(1, -1),
      gc.reshape(1, -1), w_o)


def _ffn_kernel(x_ref, mod_ref, n2g_ref, wup_ref, cw_ref, cb_ref, wdn_ref, fg_ref, o_ref,
                f_s, gc_s):
    t = pl.program_id(1)
    tile = x_ref.shape[1]

    @pl.when(t == 0)
    def _():
        gc_s[...] = jnp.zeros(gc_s.shape, F32)

    x = x_ref[0]
    mod = mod_ref[0]
    h = _rms_norm(x, n2g_ref[...]) * (1.0 + mod[4:5]) + mod[3:4]
    hb = h.astype(BF16)

    for c in range(D_FF // FF_CHUNK):
        cols = pl.ds(c * FF_CHUNK, FF_CHUNK)
        gate = jnp.dot(hb, wup_ref[:, cols], preferred_element_type=F32)
        val = jnp.dot(hb, wup_ref[:, pl.ds(D_FF + c * FF_CHUNK, FF_CHUNK)], preferred_element_type=F32)
        conv = _causal_conv3(gate, gc_s[:, cols], cw_ref[:, cols]) + cb_ref[:, cols]
        gc_s[:, cols] = gate[tile - V7X_SUBLANES:tile]
        f_s[:, cols] = (jax.nn.silu(conv) * val).astype(BF16)

    x2 = x + mod[5:6] * jnp.dot(f_s[...], wdn_ref[...], preferred_element_type=F32)
    o_ref[0] = _rms_norm(x2, fg_ref[...])


def _channel_mixer(x, mod, norm2_g, w_up, conv_w, conv_b, w_down, final_g):
    bsz, seq, _ = x.shape
    tile = SEQ_TILE
    assert seq % tile == 0 and D_FF % FF_CHUNK == 0
    const = lambda shape: pl.BlockSpec(shape, lambda b, t: (0,) * len(shape),
                                       pipeline_mode=pl.Buffered(1))
    resident = (2 * 2 * tile * D_MODEL * 4 + 3 * D_MODEL * D_FF * 2 + tile * D_FF * 2
                + 4 * tile * FF_CHUNK * 4)
    return pl.pallas_call(
        _ffn_kernel,
        grid=(bsz, seq // tile),
        in_specs=[
            pl.BlockSpec((1, tile, D_MODEL), lambda b, t: (b, t, 0)),
            pl.BlockSpec((1, N_MOD, D_MODEL), lambda b, t: (b, 0, 0)),
            const((1, D_MODEL)),
            const((D_MODEL, 2 * D_FF)),
            const((3, D_FF)),
            const((1, D_FF)),
            const((D_FF, D_MODEL)),
            const((1, D_MODEL)),
        ],
        out_specs=pl.BlockSpec((1, tile, D_MODEL), lambda b, t: (b, t, 0)),
        out_shape=jax.ShapeDtypeStruct(x.shape, F32),
        scratch_shapes=[
            pltpu.VMEM((tile, D_FF), BF16),
            pltpu.VMEM((V7X_SUBLANES, D_FF), F32),
        ],
        compiler_params=pltpu.CompilerParams(
            dimension_semantics=("arbitrary", "arbitrary"),
            vmem_limit_bytes=_vmem_limit(resident)),
        name="channel_mixer",
    )(x, mod, norm2_g.reshape(1, -1), w_up, conv_w, conv_b.reshape(1, -1), w_down,
      final_g.reshape(1, -1))


def kernel(x, c, positions, ada_w, ada_b, norm1_g, w_in, b_in, conv_w, attn_sinks, out_norm_attn_g, out_norm_conv_g, w_o, norm2_g, w_up, ffn_conv_w, ffn_conv_b, w_down, final_norm_g):
    depth = ada_w.shape[0]
    bsz = x.shape[0]
    for l in range(depth):
        mod = _modulation(c, ada_w[l], ada_b[l]).reshape(bsz, N_MOD, D_MODEL)
        x = _token_mixer(x, positions, mod, norm1_g[l], w_in[l].astype(BF16), b_in[l], conv_w[l],
                         attn_sinks[l], out_norm_attn_g[l], out_norm_conv_g[l], w_o[l].astype(BF16))
        last = l == depth - 1
        assert last, "the final RMSNorm is fused into the last layer's channel mixer"
        x = _channel_mixer(x, mod, norm2_g[l], w_up[l].astype(BF16), ffn_conv_w[l], ffn_conv_b[l],
                           w_down[l].astype(BF16), final_norm_g)
    return x
```

```python
import functools
import math

import jax
import jax.numpy as jnp
from jax import lax
from jax.experimental import pallas as pl
from jax.experimental.pallas import tpu as pltpu

D_MODEL = 1024
HEAD_DIM = 64
N_Q_HEADS = 8
N_KV_HEADS = 2
D_ATTN = N_Q_HEADS * HEAD_DIM
D_KV = N_KV_HEADS * HEAD_DIM
D_CONV = D_MODEL - D_ATTN
D_QKV = D_ATTN + 2 * D_KV
D_IN = D_QKV + 3 * D_CONV
WINDOW = 128
D_FF = 2816
N_MOD = 6
ROPE_THETA = 10000.0
EPS = 1e-6
MASK_VALUE = -1e30

V7X_LANES = 128
V7X_SUBLANES = 8
V7X_MXU_COLS = 256
V7X_VMEM_BYTES = 64 * 1024 * 1024

SEQ_TILE = 1024
FF_CHUNK = V7X_MXU_COLS
PAIR = 2 * HEAD_DIM

F32 = jnp.float32
BF16 = jnp.bfloat16


def _vmem_limit(resident_bytes):
    return min(2 * resident_bytes, V7X_VMEM_BYTES - 4 * 1024 * 1024)


def _rms_norm(x, g):
    ms = jnp.mean(x * x, axis=-1, keepdims=True)
    return x * lax.rsqrt(ms + EPS) * g


def _causal_conv3(cur, prev, w):
    rows = lax.broadcasted_iota(jnp.int32, cur.shape, 0)
    p1 = prev[V7X_SUBLANES - 1:V7X_SUBLANES]
    p2 = prev[V7X_SUBLANES - 2:V7X_SUBLANES - 1]
    r1 = jnp.where(rows == 0, p1, pltpu.roll(cur, 1, 0))
    r2 = jnp.where(rows == 0, p2, jnp.where(rows == 1, p1, pltpu.roll(cur, 2, 0)))
    return r2 * w[0:1] + r1 * w[1:2] + cur * w[2:3]


def _mod_kernel(c_ref, w_ref, b_ref, o_ref):
    c_act = jax.nn.silu(c_ref[...])
    o_ref[...] = jnp.dot(c_act.astype(BF16), w_ref[...].astype(BF16),
                         preferred_element_type=F32) + b_ref[...]


def _modulation(c, ada_w, ada_b):
    bsz = c.shape[0]
    n = ada_w.shape[1]
    blk = D_MODEL
    return pl.pallas_call(
        _mod_kernel,
        grid=(n // blk,),
        in_specs=[
            pl.BlockSpec((bsz, D_MODEL), lambda j: (0, 0)),
            pl.BlockSpec((D_MODEL, blk), lambda j: (0, j)),
            pl.BlockSpec((1, blk), lambda j: (0, j)),
        ],
        out_specs=pl.BlockSpec((bsz, blk), lambda j: (0, j)),
        out_shape=jax.ShapeDtypeStruct((bsz, n), F32),
        name="adaln_modulation",
    )(c, ada_w, ada_b.reshape(1, n))


def _mixer_kernel(sinks_ref, x_ref, pos_ref, mod_ref, invf_ref, n1g_ref, win_ref, bin_ref,
                  cw_ref, ga_ref, gc_ref, wo_ref, o_ref,
                  q_s, k_s, v_s, ya_s, cc_s, s_s, m_s, p_s, t_s, y_s):
    t = pl.program_id(1)
    tile = x_ref.shape[1]
    nblk = tile // WINDOW

    lane = lax.broadcasted_iota(jnp.int32, (1, PAIR), 1)
    low_head = lane < HEAD_DIM
    first_half = (lane % HEAD_DIM) < (HEAD_DIM // 2)

    @pl.when(t > 0)
    def _():
        k_s[:, 0:WINDOW, :] = k_s[:, tile:tile + WINDOW, :]
        v_s[:, 0:WINDOW, 0:PAIR] = v_s[:, tile:tile + WINDOW, 0:PAIR]

    @pl.when(t == 0)
    def _():
        k_s[:, 0:WINDOW, :] = jnp.zeros((N_KV_HEADS, WINDOW, PAIR), BF16)
        v_s[:, 0:WINDOW, 0:PAIR] = jnp.zeros((2 * N_KV_HEADS, WINDOW, PAIR), BF16)
        ones_low = jnp.broadcast_to(jnp.where(low_head, 1.0, 0.0).astype(BF16), (tile + WINDOW, PAIR))
        ones_high = jnp.broadcast_to(jnp.where(low_head, 0.0, 1.0).astype(BF16), (tile + WINDOW, PAIR))
        for g in range(N_KV_HEADS):
            v_s[2 * g, :, PAIR:] = ones_low
            v_s[2 * g + 1, :, PAIR:] = ones_high
        cc_s[...] = jnp.zeros(cc_s.shape, F32)

    x = x_ref[0]
    mod = mod_ref[0]
    h = _rms_norm(x, n1g_ref[...]) * (1.0 + mod[1:2]) + mod[0:1]
    hb = h.astype(BF16)

    qkv = jnp.dot(hb, win_ref[:, 0:D_QKV], preferred_element_type=F32) + bin_ref[:, 0:D_QKV]

    ang = invf_ref[...] * pos_ref[0].astype(F32)
    cos_f, sin_f = jnp.cos(ang), jnp.sin(ang)
    cos_t = jnp.concatenate([cos_f, cos_f, cos_f, cos_f], axis=0).T
    sin_t = jnp.concatenate([-sin_f, sin_f, -sin_f, sin_f], axis=0).T

    def rope(u):
        rot = jnp.where(first_half, pltpu.roll(u, PAIR - HEAD_DIM // 2, 1),
                        pltpu.roll(u, HEAD_DIM // 2, 1))
        return u * cos_t + rot * sin_t

    scale = 1.0 / math.sqrt(HEAD_DIM)
    for p in range(D_ATTN // PAIR):
        q_s[:, p * PAIR:(p + 1) * PAIR] = (rope(qkv[:, p * PAIR:(p + 1) * PAIR]) * scale).astype(BF16)

    k = rope(qkv[:, D_ATTN:D_ATTN + D_KV])
    k_sw = pltpu.roll(k, HEAD_DIM, 1)
    v = qkv[:, D_ATTN + D_KV:D_QKV]
    v_sw = pltpu.roll(v, HEAD_DIM, 1)
    cur = pl.ds(WINDOW, tile)
    k_s[0, cur, :] = jnp.where(low_head, k, k_sw).astype(BF16)
    k_s[1, cur, :] = jnp.where(low_head, k_sw, k).astype(BF16)
    v_s[0, cur, 0:PAIR] = jnp.where(low_head, v, 0.0).astype(BF16)
    v_s[1, cur, 0:PAIR] = jnp.where(low_head, 0.0, v_sw).astype(BF16)
    v_s[2, cur, 0:PAIR] = jnp.where(low_head, v_sw, 0.0).astype(BF16)
    v_s[3, cur, 0:PAIR] = jnp.where(low_head, 0.0, v).astype(BF16)

    cproj = jnp.dot(hb, win_ref[:, D_QKV:D_IN], preferred_element_type=F32) + bin_ref[:, D_QKV:D_IN]
    bx = cproj[:, 0:D_CONV] * cproj[:, 2 * D_CONV:3 * D_CONV]
    y_conv = cproj[:, D_CONV:2 * D_CONV] * _causal_conv3(bx, cc_s[...], cw_ref[...])
    cc_s[...] = bx[tile - V7X_SUBLANES:tile]
    y_s[:, D_ATTN:] = _rms_norm(y_conv, gc_ref[...]).astype(BF16)

    qi = lax.broadcasted_iota(jnp.int32, (WINDOW, 2 * WINDOW), 0)
    kj = lax.broadcasted_iota(jnp.int32, (WINDOW, 2 * WINDOW), 1)
    diff = qi + WINDOW - kj
    band = (diff >= 0) & (diff < WINDOW)
    band_first = band & ((kj >= WINDOW) | (t > 0))
    group = N_Q_HEADS // N_KV_HEADS

    units = [(i, g) for i in range(nblk) for g in range(N_KV_HEADS)]

    for u, (i, g) in enumerate(units):
        mask = band_first if i == 0 else band
        pieces = []
        for p in range(g * group // 2, (g + 1) * group // 2):
            qp = q_s[pl.ds(i * WINDOW, WINDOW), p * PAIR:(p + 1) * PAIR]
            zero = jnp.zeros_like(qp)
            pieces += [jnp.where(low_head, qp, zero), jnp.where(low_head, zero, qp)]
        s = lax.dot_general(jnp.concatenate(pieces, axis=0), k_s[g, pl.ds(i * WINDOW, 2 * WINDOW), :],
                            (((1,), (1,)), ((), ())), preferred_element_type=F32)
        for j in range(group):
            sj = jnp.where(mask, s[j * WINDOW:(j + 1) * WINDOW], MASK_VALUE)
            s_s[u * group + j] = sj
            m = jnp.maximum(jnp.max(sj, axis=-1, keepdims=True), sinks_ref[g * group + j])
            m_s[u * group + j] = jnp.broadcast_to(m, (WINDOW, PAIR))

    for u, (i, g) in enumerate(units):
        for j in range(group):
            m = m_s[u * group + j]
            e = jnp.exp(s_s[u * group + j] - jnp.concatenate([m, m], axis=1))
            p_s[u, pl.ds((j // 2) * WINDOW, WINDOW), pl.ds((j % 2) * 2 * WINDOW, 2 * WINDOW)] = e.astype(BF16)
        for jj in range(group // 2):
            sink_lo = jnp.exp(sinks_ref[g * group + 2 * jj] - m_s[u * group + 2 * jj])
            sink_hi = jnp.exp(sinks_ref[g * group + 2 * jj + 1] - m_s[u * group + 2 * jj + 1])
            t_s[u * (group // 2) + jj] = jnp.where(low_head, sink_lo, sink_hi)

    for u, (i, g) in enumerate(units):
        krows = pl.ds(i * WINDOW, 2 * WINDOW)
        rhs = jnp.concatenate([v_s[2 * g, krows, :], v_s[2 * g + 1, krows, :]], axis=0)
        o = jnp.dot(p_s[u], rhs, preferred_element_type=F32)
        for jj in range(group // 2):
            oj = o[jj * WINDOW:(jj + 1) * WINDOW]
            den = oj[:, PAIR:] + t_s[u * (group // 2) + jj]
            p = g * group // 2 + jj
            ya_s[pl.ds(i * WINDOW, WINDOW), p * PAIR:(p + 1) * PAIR] = oj[:, 0:PAIR] / den

    y_s[:, 0:D_ATTN] = _rms_norm(ya_s[...], ga_ref[...]).astype(BF16)
    o_ref[0] = x + mod[2:3] * jnp.dot(y_s[...], wo_ref[...], preferred_element_type=F32)


def _token_mixer(x, positions, mod, norm1_g, w_in, b_in, conv_w, sinks, ga, gc, w_o):
    bsz, seq, _ = x.shape
    tile = SEQ_TILE
    assert seq % tile == 0 and tile % WINDOW == 0
    inv_freq = ROPE_THETA ** (-jnp.arange(0, HEAD_DIM // 2, dtype=F32) / (HEAD_DIM // 2))
    const = lambda shape: pl.BlockSpec(shape, lambda b, t: (0,) * len(shape),
                                       pipeline_mode=pl.Buffered(1))
    nhb = (tile // WINDOW) * N_Q_HEADS
    resident = (2 * 2 * tile * D_MODEL * 4 + (D_MODEL * D_IN + D_MODEL * D_MODEL) * 2
                + tile * D_IN * 4 + tile * (D_ATTN * 6 + 6 * PAIR * 2)
                + nhb * WINDOW * WINDOW * (8 + 4 + 2 + 2))
    return pl.pallas_call(
        _mixer_kernel,
        grid=(bsz, seq // tile),
        in_specs=[
            pl.BlockSpec(memory_space=pltpu.SMEM),
            pl.BlockSpec((1, tile, D_MODEL), lambda b, t: (b, t, 0)),
            pl.BlockSpec((1, 1, tile), lambda b, t: (b, 0, t)),
            pl.BlockSpec((1, N_MOD, D_MODEL), lambda b, t: (b, 0, 0)),
            const((HEAD_DIM // 2, 1)),
            const((1, D_MODEL)),
            const((D_MODEL, D_IN)),
            const((1, D_IN)),
            const((3, D_CONV)),
            const((1, D_ATTN)),
            const((1, D_CONV)),
            const((D_MODEL, D_MODEL)),
        ],
        out_specs=pl.BlockSpec((1, tile, D_MODEL), lambda b, t: (b, t, 0)),
        out_shape=jax.ShapeDtypeStruct(x.shape, F32),
        scratch_shapes=[
            pltpu.VMEM((tile, D_ATTN), BF16),
            pltpu.VMEM((N_KV_HEADS, tile + WINDOW, PAIR), BF16),
            pltpu.VMEM((2 * N_KV_HEADS, tile + WINDOW, 2 * PAIR), BF16),
            pltpu.VMEM((tile, D_ATTN), F32),
            pltpu.VMEM((V7X_SUBLANES, D_CONV), F32),
            pltpu.VMEM((nhb, WINDOW, 2 * WINDOW), F32),
            pltpu.VMEM((nhb, WINDOW, PAIR), F32),
            pltpu.VMEM((nhb // 4, 2 * WINDOW, 4 * WINDOW), BF16),
            pltpu.VMEM((nhb // 2, WINDOW, PAIR), F32),
            pltpu.VMEM((tile, D_MODEL), BF16),
        ],
        compiler_params=pltpu.CompilerParams(
            dimension_semantics=("arbitrary", "arbitrary"),
            vmem_limit_bytes=_vmem_limit(resident)),
        name="token_mixer",
    )(sinks, x, positions.reshape(bsz, 1, seq), mod, inv_freq.reshape(-1, 1),
      norm1_g.reshape(1, -1), w_in, b_in.reshape(1, -1), conv_w, ga.reshape(1, -1),
      gc.reshape(1, -1), w_o)


def _ffn_kernel(x_ref, mod_ref, n2g_ref, wup_ref, cw_ref, cb_ref, wdn_ref, fg_ref, o_ref,
                f_s, gc_s):
    t = pl.program_id(1)
    tile = x_ref.shape[1]

    @pl.when(t == 0)
    def _():
        gc_s[...] = jnp.zeros(gc_s.shape, F32)

    x = x_ref[0]
    mod = mod_ref[0]
    h = _rms_norm(x, n2g_ref[...]) * (1.0 + mod[4:5]) + mod[3:4]
    hb = h.astype(BF16)

    for c in range(D_FF // FF_CHUNK):
        cols = pl.ds(c * FF_CHUNK, FF_CHUNK)
        gate = jnp.dot(hb, wup_ref[:, cols], preferred_element_type=F32)
        val = jnp.dot(hb, wup_ref[:, pl.ds(D_FF + c * FF_CHUNK, FF_CHUNK)], preferred_element_type=F32)
        conv = _causal_conv3(gate, gc_s[:, cols], cw_ref[:, cols]) + cb_ref[:, cols]
        gc_s[:, cols] = gate[tile - V7X_SUBLANES:tile]
        f_s[:, cols] = (jax.nn.silu(conv) * val).astype(BF16)

    x2 = x + mod[5:6] * jnp.dot(f_s[...], wdn_ref[...], preferred_element_type=F32)
    o_ref[0] = _rms_norm(x2, fg_ref[...])


def _channel_mixer(x, mod, norm2_g, w_up, conv_w, conv_b, w_down, final_g):
    bsz, seq, _ = x.shape
    tile = SEQ_TILE
    assert seq % tile == 0 and D_FF % FF_CHUNK == 0
    const = lambda shape: pl.BlockSpec(shape, lambda b, t: (0,) * len(shape),
                                       pipeline_mode=pl.Buffered(1))
    resident = (2 * 2 * tile * D_MODEL * 4 + 3 * D_MODEL * D_FF * 2 + tile * D_FF * 2
                + 4 * tile * FF_CHUNK * 4)
    return pl.pallas_call(
        _ffn_kernel,
        grid=(bsz, seq // tile),
        in_specs=[
            pl.BlockSpec((1, tile, D_MODEL), lambda b, t: (b, t, 0)),
            pl.BlockSpec((1, N_MOD, D_MODEL), lambda b, t: (b, 0, 0)),
            const((1, D_MODEL)),
            const((D_MODEL, 2 * D_FF)),
            const((3, D_FF)),
            const((1, D_FF)),
            const((D_FF, D_MODEL)),
            const((1, D_MODEL)),
        ],
        out_specs=pl.BlockSpec((1, tile, D_MODEL), lambda b, t: (b, t, 0)),
        out_shape=jax.ShapeDtypeStruct(x.shape, F32),
        scratch_shapes=[
            pltpu.VMEM((tile, D_FF), BF16),
            pltpu.VMEM((V7X_SUBLANES, D_FF), F32),
        ],
        compiler_params=pltpu.CompilerParams(
            dimension_semantics=("arbitrary", "arbitrary"),
            vmem_limit_bytes=_vmem_limit(resident)),
        name="channel_mixer",
    )(x, mod, norm2_g.reshape(1, -1), w_up, conv_w, conv_b.reshape(1, -1), w_down,
      final_g.reshape(1, -1))


def kernel(x, c, positions, ada_w, ada_b, norm1_g, w_in, b_in, conv_w, attn_sinks, out_norm_attn_g, out_norm_conv_g, w_o, norm2_g, w_up, ffn_conv_w, ffn_conv_b, w_down, final_norm_g):
    depth = ada_w.shape[0]
    bsz = x.shape[0]
    for l in range(depth):
        mod = _modulation(c, ada_w[l], ada_b[l]).reshape(bsz, N_MOD, D_MODEL)
        x = _token_mixer(x, positions, mod, norm1_g[l], w_in[l].astype(BF16), b_in[l], conv_w[l],
                         attn_sinks[l], out_norm_attn_g[l], out_norm_conv_g[l], w_o[l].astype(BF16))
        last = l == depth - 1
        assert last, "the final RMSNorm is fused into the last layer's channel mixer"
        x = _channel_mixer(x, mod, norm2_g[l], w_up[l].astype(BF16), ffn_conv_w[l], ffn_conv_b[l],
                           w_down[l].astype(BF16), final_norm_g)
    return x
```

```python
import functools
import math

import jax
import jax.numpy as jnp
from jax import lax
from jax.experimental import pallas as pl
from jax.experimental.pallas import tpu as pltpu

D_MODEL = 1024
HEAD_DIM = 64
N_Q_HEADS = 8
N_KV_HEADS = 2
D_ATTN = N_Q_HEADS * HEAD_DIM
D_KV = N_KV_HEADS * HEAD_DIM
D_CONV = D_MODEL - D_ATTN
D_QKV = D_ATTN + 2 * D_KV
D_IN = D_QKV + 3 * D_CONV
WINDOW = 128
D_FF = 2816
N_MOD = 6
ROPE_THETA = 10000.0
EPS = 1e-6
MASK_VALUE = -1e30
LOG2_E = math.log2(math.e)

V7X_LANES = 128
V7X_SUBLANES = 8
V7X_MXU_COLS = 256
V7X_VMEM_BYTES = 64 * 1024 * 1024

SEQ_TILE = 1024
FF_CHUNK = V7X_MXU_COLS
PAIR = 2 * HEAD_DIM
PV_LAG = 2

F32 = jnp.float32
BF16 = jnp.bfloat16


def _vmem_limit(resident_bytes):
    return min(2 * resident_bytes, V7X_VMEM_BYTES - 4 * 1024 * 1024)


def _rms_norm(x, g):
    ms = jnp.mean(x * x, axis=-1, keepdims=True)
    return x * lax.rsqrt(ms + EPS) * g


def _causal_conv3(cur, prev, w):
    rows = lax.broadcasted_iota(jnp.int32, cur.shape, 0)
    p1 = prev[V7X_SUBLANES - 1:V7X_SUBLANES]
    p2 = prev[V7X_SUBLANES - 2:V7X_SUBLANES - 1]
    r1 = jnp.where(rows == 0, p1, pltpu.roll(cur, 1, 0))
    r2 = jnp.where(rows == 0, p2, jnp.where(rows == 1, p1, pltpu.roll(cur, 2, 0)))
    return r2 * w[0:1] + r1 * w[1:2] + cur * w[2:3]


def _mod_kernel(c_ref, w_ref, b_ref, o_ref):
    c_act = jax.nn.silu(c_ref[...])
    o_ref[...] = jnp.dot(c_act.astype(BF16), w_ref[...].astype(BF16),
                         preferred_element_type=F32) + b_ref[...]


def _modulation(c, ada_w, ada_b):
    bsz = c.shape[0]
    n = ada_w.shape[1]
    blk = D_MODEL
    return pl.pallas_call(
        _mod_kernel,
        grid=(n // blk,),
        in_specs=[
            pl.BlockSpec((bsz, D_MODEL), lambda j: (0, 0)),
            pl.BlockSpec((D_MODEL, blk), lambda j: (0, j)),
            pl.BlockSpec((1, blk), lambda j: (0, j)),
        ],
        out_specs=pl.BlockSpec((bsz, blk), lambda j: (0, j)),
        out_shape=jax.ShapeDtypeStruct((bsz, n), F32),
        name="adaln_modulation",
    )(c, ada_w, ada_b.reshape(1, n))


def _mixer_kernel(sinks_ref, x_ref, pos_ref, mod_ref, invf_ref, n1g_ref, win_ref, bin_ref,
                  cw_ref, ga_ref, gc_ref, wo_ref, o_ref,
                  q_s, k_s, v_s, ya_s, cc_s, s_s, m_s, p_s, t_s, y_s):
    t = pl.program_id(1)
    tile = x_ref.shape[1]
    nblk = tile // WINDOW

    lane = lax.broadcasted_iota(jnp.int32, (1, PAIR), 1)
    low_head = lane < HEAD_DIM
    first_half = (lane % HEAD_DIM) < (HEAD_DIM // 2)

    @pl.when(t > 0)
    def _():
        k_s[:, 0:WINDOW, :] = k_s[:, tile:tile + WINDOW, :]
        v_s[:, 0:WINDOW, 0:PAIR] = v_s[:, tile:tile + WINDOW, 0:PAIR]

    @pl.when(t == 0)
    def _():
        k_s[:, 0:WINDOW, :] = jnp.zeros((N_KV_HEADS, WINDOW, PAIR), BF16)
        v_s[:, 0:WINDOW, 0:PAIR] = jnp.zeros((2 * N_KV_HEADS, WINDOW, PAIR), BF16)
        ones_low = jnp.broadcast_to(jnp.where(low_head, 1.0, 0.0).astype(BF16), (tile + WINDOW, PAIR))
        ones_high = jnp.broadcast_to(jnp.where(low_head, 0.0, 1.0).astype(BF16), (tile + WINDOW, PAIR))
        for g in range(N_KV_HEADS):
            v_s[2 * g, :, PAIR:] = ones_low
            v_s[2 * g + 1, :, PAIR:] = ones_high
        cc_s[...] = jnp.zeros(cc_s.shape, F32)

    x = x_ref[0]
    mod = mod_ref[0]
    h = _rms_norm(x, n1g_ref[...]) * (1.0 + mod[1:2]) + mod[0:1]
    hb = h.astype(BF16)

    qkv = jnp.dot(hb, win_ref[:, 0:D_QKV], preferred_element_type=F32) + bin_ref[:, 0:D_QKV]

    ang = invf_ref[...] * pos_ref[0].astype(F32)
    cos_f, sin_f = jnp.cos(ang), jnp.sin(ang)
    cos_t = jnp.concatenate([cos_f, cos_f, cos_f, cos_f], axis=0).T
    sin_t = jnp.concatenate([-sin_f, sin_f, -sin_f, sin_f], axis=0).T

    def rope(u):
        rot = jnp.where(first_half, pltpu.roll(u, PAIR - HEAD_DIM // 2, 1),
                        pltpu.roll(u, HEAD_DIM // 2, 1))
        return u * cos_t + rot * sin_t

    scale = LOG2_E / math.sqrt(HEAD_DIM)
    for p in range(D_ATTN // PAIR):
        q_s[:, p * PAIR:(p + 1) * PAIR] = (rope(qkv[:, p * PAIR:(p + 1) * PAIR]) * scale).astype(BF16)

    k = rope(qkv[:, D_ATTN:D_ATTN + D_KV])
    k_sw = pltpu.roll(k, HEAD_DIM, 1)
    v = qkv[:, D_ATTN + D_KV:D_QKV]
    v_sw = pltpu.roll(v, HEAD_DIM, 1)
    cur = pl.ds(WINDOW, tile)
    k_s[0, cur, :] = jnp.where(low_head, k, k_sw).astype(BF16)
    k_s[1, cur, :] = jnp.where(low_head, k_sw, k).astype(BF16)
    v_s[0, cur, 0:PAIR] = jnp.where(low_head, v, 0.0).astype(BF16)
    v_s[1, cur, 0:PAIR] = jnp.where(low_head, 0.0, v_sw).astype(BF16)
    v_s[2, cur, 0:PAIR] = jnp.where(low_head, v_sw, 0.0).astype(BF16)
    v_s[3, cur, 0:PAIR] = jnp.where(low_head, 0.0, v).astype(BF16)

    cproj = jnp.dot(hb, win_ref[:, D_QKV:D_IN], preferred_element_type=F32) + bin_ref[:, D_QKV:D_IN]
    bx = cproj[:, 0:D_CONV] * cproj[:, 2 * D_CONV:3 * D_CONV]
    y_conv = cproj[:, D_CONV:2 * D_CONV] * _causal_conv3(bx, cc_s[...], cw_ref[...])
    cc_s[...] = bx[tile - V7X_SUBLANES:tile]
    y_s[:, D_ATTN:] = _rms_norm(y_conv, gc_ref[...]).astype(BF16)

    kj = lax.broadcasted_iota(jnp.int32, (2 * WINDOW, WINDOW), 0)
    qi = lax.broadcasted_iota(jnp.int32, (2 * WINDOW, WINDOW), 1)
    diff = qi + WINDOW - kj
    band = (diff >= 0) & (diff < WINDOW)
    bias = jnp.where(band, 0.0, MASK_VALUE).astype(BF16)
    bias_first = jnp.where(band & ((kj >= WINDOW) | (t > 0)), 0.0, MASK_VALUE).astype(BF16)
    one_hot = (lax.broadcasted_iota(jnp.int32, (N_Q_HEADS // N_KV_HEADS * WINDOW, WINDOW), 0) % WINDOW
               == lax.broadcasted_iota(jnp.int32, (N_Q_HEADS // N_KV_HEADS * WINDOW, WINDOW), 1)).astype(BF16)
    group = N_Q_HEADS // N_KV_HEADS

    units = [(i, g) for i in range(nblk) for g in range(N_KV_HEADS)]

    for u, (i, g) in enumerate(units):
        pieces = []
        for p in range(g * group // 2, (g + 1) * group // 2):
            qp = q_s[pl.ds(i * WINDOW, WINDOW), p * PAIR:(p + 1) * PAIR]
            zero = jnp.zeros_like(qp)
            pieces += [jnp.where(low_head, qp, zero), jnp.where(low_head, zero, qp)]
        lhs = jnp.concatenate([jnp.concatenate(pieces, axis=0), one_hot], axis=1)
        rhs = jnp.concatenate([k_s[g, pl.ds(i * WINDOW, 2 * WINDOW), :],
                               bias_first if i == 0 else bias], axis=1)
        s = lax.dot_general(lhs, rhs, (((1,), (1,)), ((), ())), preferred_element_type=F32)
        for j in range(group):
            sj = s[j * WINDOW:(j + 1) * WINDOW]
            s_s[u * group + j] = sj
            m = jnp.maximum(jnp.max(sj, axis=-1, keepdims=True), sinks_ref[g * group + j] * LOG2_E)
            m_s[u * group + j] = jnp.broadcast_to(m, (WINDOW, PAIR))

    def exponentials(u, i, g):
        for j in range(group):
            m = m_s[u * group + j]
            e = jnp.exp2(s_s[u * group + j] - jnp.concatenate([m, m], axis=1))
            p_s[u, pl.ds((j // 2) * WINDOW, WINDOW), pl.ds((j % 2) * 2 * WINDOW, 2 * WINDOW)] = e.astype(BF16)
        for jj in range(group // 2):
            sink_lo = jnp.exp2(sinks_ref[g * group + 2 * jj] * LOG2_E - m_s[u * group + 2 * jj])
            sink_hi = jnp.exp2(sinks_ref[g * group + 2 * jj + 1] * LOG2_E - m_s[u * group + 2 * jj + 1])
            t_s[u * (group // 2) + jj] = jnp.where(low_head, sink_lo, sink_hi)

    def weighted_values(u, i, g):
        krows = pl.ds(i * WINDOW, 2 * WINDOW)
        rhs = jnp.concatenate([v_s[2 * g, krows, :], v_s[2 * g + 1, krows, :]], axis=0)
        o = jnp.dot(p_s[u], rhs, preferred_element_type=F32)
        for jj in range(group // 2):
            oj = o[jj * WINDOW:(jj + 1) * WINDOW]
            den = oj[:, PAIR:] + t_s[u * (group // 2) + jj]
            p = g * group // 2 + jj
            ya_s[pl.ds(i * WINDOW, WINDOW), p * PAIR:(p + 1) * PAIR] = oj[:, 0:PAIR] / den

    for u in range(len(units) + PV_LAG):
        if u < len(units):
            exponentials(u, *units[u])
        if u >= PV_LAG:
            weighted_values(u - PV_LAG, *units[u - PV_LAG])

    y_s[:, 0:D_ATTN] = _rms_norm(ya_s[...], ga_ref[...]).astype(BF16)
    o_ref[0] = x_ref[0] + mod[2:3] * jnp.dot(y_s[...], wo_ref[...], preferred_element_type=F32)


def _token_mixer(x, positions, mod, norm1_g, w_in, b_in, conv_w, sinks, ga, gc, w_o):
    bsz, seq, _ = x.shape
    tile = SEQ_TILE
    assert seq % tile == 0 and tile % WINDOW == 0
    inv_freq = ROPE_THETA ** (-jnp.arange(0, HEAD_DIM // 2, dtype=F32) / (HEAD_DIM // 2))
    const = lambda shape: pl.BlockSpec(shape, lambda b, t: (0,) * len(shape),
                                       pipeline_mode=pl.Buffered(1))
    nhb = (tile // WINDOW) * N_Q_HEADS
    resident = (2 * 2 * tile * D_MODEL * 4 + (D_MODEL * D_IN + D_MODEL * D_MODEL) * 2
                + tile * D_IN * 4 + tile * (D_ATTN * 6 + 6 * PAIR * 2)
                + nhb * WINDOW * WINDOW * (8 + 4 + 2 + 2))
    return pl.pallas_call(
        _mixer_kernel,
        grid=(bsz, seq // tile),
        in_specs=[
            pl.BlockSpec(memory_space=pltpu.SMEM),
            pl.BlockSpec((1, tile, D_MODEL), lambda b, t: (b, t, 0)),
            pl.BlockSpec((1, 1, tile), lambda b, t: (b, 0, t)),
            pl.BlockSpec((1, N_MOD, D_MODEL), lambda b, t: (b, 0, 0)),
            const((HEAD_DIM // 2, 1)),
            const((1, D_MODEL)),
            const((D_MODEL, D_IN)),
            const((1, D_IN)),
            const((3, D_CONV)),
            const((1, D_ATTN)),
            const((1, D_CONV)),
            const((D_MODEL, D_MODEL)),
        ],
        out_specs=pl.BlockSpec((1, tile, D_MODEL), lambda b, t: (b, t, 0)),
        out_shape=jax.ShapeDtypeStruct(x.shape, F32),
        scratch_shapes=[
            pltpu.VMEM((tile, D_ATTN), BF16),
            pltpu.VMEM((N_KV_HEADS, tile + WINDOW, PAIR), BF16),
            pltpu.VMEM((2 * N_KV_HEADS, tile + WINDOW, 2 * PAIR), BF16),
            pltpu.VMEM((tile, D_ATTN), F32),
            pltpu.VMEM((V7X_SUBLANES, D_CONV), F32),
            pltpu.VMEM((nhb, WINDOW, 2 * WINDOW), F32),
            pltpu.VMEM((nhb, WINDOW, PAIR), F32),
            pltpu.VMEM((nhb // 4, 2 * WINDOW, 4 * WINDOW), BF16),
            pltpu.VMEM((nhb // 2, WINDOW, PAIR), F32),
            pltpu.VMEM((tile, D_MODEL), BF16),
        ],
        compiler_params=pltpu.CompilerParams(
            dimension_semantics=("arbitrary", "arbitrary"),
            vmem_limit_bytes=_vmem_limit(resident)),
        name="token_mixer",
    )(sinks, x, positions.reshape(bsz, 1, seq), mod, inv_freq.reshape(-1, 1),
      norm1_g.reshape(1, -1), w_in, b_in.reshape(1, -1), conv_w, ga.reshape(1, -1),
      gc.reshape(1, -1), w_o)


F_OWN = (D_FF // D_MODEL) * D_MODEL


def _ffn_kernel(x_ref, xn_ref, mod_ref, modn_ref, n2g_ref, wup_ref, cw_ref, cb_ref, wdn_ref, fg_ref,
                o_ref, h_s, f_s, gc_s):
    t = pl.program_id(1)
    step = pl.program_id(0) * pl.num_programs(1) + t
    tile = x_ref.shape[1]
    slot = step % 2

    def modulated_norm(xr, mr):
        m = mr[0]
        return (_rms_norm(xr[0], n2g_ref[...]) * (1.0 + m[4:5]) + m[3:4]).astype(BF16)

    @pl.when(step == 0)
    def _():
        h_s[0] = modulated_norm(x_ref, mod_ref)

    @pl.when(t == 0)
    def _():
        gc_s[...] = jnp.zeros(gc_s.shape, F32)

    hb = h_s[slot]
    for c in range(D_FF // FF_CHUNK):
        cols = pl.ds(c * FF_CHUNK, FF_CHUNK)
        gate = jnp.dot(hb, wup_ref[:, cols], preferred_element_type=F32)
        val = jnp.dot(hb, wup_ref[:, pl.ds(D_FF + c * FF_CHUNK, FF_CHUNK)], preferred_element_type=F32)
        conv = _causal_conv3(gate, gc_s[:, cols], cw_ref[:, cols]) + cb_ref[:, cols]
        gc_s[:, cols] = gate[tile - V7X_SUBLANES:tile]
        f = (jax.nn.silu(conv) * val).astype(BF16)
        if c * FF_CHUNK < F_OWN:
            f_s[:, cols] = f
        else:
            h_s[2, :, pl.ds(c * FF_CHUNK - F_OWN, FF_CHUNK)] = f

    def down(cols):
        lhs = jnp.concatenate([h_s[2][:, 0:D_FF - F_OWN], f_s[...]], axis=1)
        rhs = jnp.concatenate([wdn_ref[F_OWN:D_FF, cols], wdn_ref[0:F_OWN, cols]], axis=0)
        return jnp.dot(lhs, rhs, preferred_element_type=F32)

    half = D_MODEL // 2
    down_lo = down(pl.ds(0, half))
    h_s[1 - slot] = modulated_norm(xn_ref, modn_ref)
    down_hi = down(pl.ds(half, half))
    x2 = x_ref[0] + mod_ref[0][5:6] * jnp.concatenate([down_lo, down_hi], axis=1)
    o_ref[0] = _rms_norm(x2, fg_ref[...])


def _channel_mixer(x, mod, norm2_g, w_up, conv_w, conv_b, w_down, final_g):
    bsz, seq, _ = x.shape
    tile = SEQ_TILE
    assert seq % tile == 0 and D_FF % FF_CHUNK == 0
    const = lambda shape: pl.BlockSpec(shape, lambda b, t: (0,) * len(shape),
                                       pipeline_mode=pl.Buffered(1))
    nt = seq // tile
    nxt = lambda b, t: jnp.minimum(b * nt + t + 1, bsz * nt - 1)
    resident = (3 * 2 * tile * D_MODEL * 4 + 3 * D_MODEL * D_FF * 2 + tile * (3 * D_MODEL + F_OWN) * 2
                + 4 * tile * FF_CHUNK * 4)
    return pl.pallas_call(
        _ffn_kernel,
        grid=(bsz, nt),
        in_specs=[
            pl.BlockSpec((1, tile, D_MODEL), lambda b, t: (b, t, 0)),
            pl.BlockSpec((1, tile, D_MODEL), lambda b, t: (nxt(b, t) // nt, nxt(b, t) % nt, 0)),
            pl.BlockSpec((1, N_MOD, D_MODEL), lambda b, t: (b, 0, 0)),
            pl.BlockSpec((1, N_MOD, D_MODEL), lambda b, t: (nxt(b, t) // nt, 0, 0)),
            const((1, D_MODEL)),
            const((D_MODEL, 2 * D_FF)),
            const((3, D_FF)),
            const((1, D_FF)),
            const((D_FF, D_MODEL)),
            const((1, D_MODEL)),
        ],
        out_specs=pl.BlockSpec((1, tile, D_MODEL), lambda b, t: (b, t, 0)),
        out_shape=jax.ShapeDtypeStruct(x.shape, F32),
        scratch_shapes=[
            pltpu.VMEM((3, tile, D_MODEL), BF16),
            pltpu.VMEM((tile, F_OWN), BF16),
            pltpu.VMEM((V7X_SUBLANES, D_FF), F32),
        ],
        compiler_params=pltpu.CompilerParams(
            dimension_semantics=("arbitrary", "arbitrary"),
            vmem_limit_bytes=_vmem_limit(resident)),
        name="channel_mixer",
    )(x, x, mod, mod, norm2_g.reshape(1, -1), w_up, conv_w, conv_b.reshape(1, -1), w_down,
      final_g.reshape(1, -1))


def kernel(x, c, positions, ada_w, ada_b, norm1_g, w_in, b_in, conv_w, attn_sinks, out_norm_attn_g, out_norm_conv_g, w_o, norm2_g, w_up, ffn_conv_w, ffn_conv_b, w_down, final_norm_g):
    depth = ada_w.shape[0]
    bsz = x.shape[0]
    for l in range(depth):
        mod = _modulation(c, ada_w[l], ada_b[l]).reshape(bsz, N_MOD, D_MODEL)
        x = _token_mixer(x, positions, mod, norm1_g[l], w_in[l].astype(BF16), b_in[l], conv_w[l],
                         attn_sinks[l], out_norm_attn_g[l], out_norm_conv_g[l], w_o[l].astype(BF16))
        last = l == depth - 1
        assert last, "the final RMSNorm is fused into the last layer's channel mixer"
        x = _channel_mixer(x, mod, norm2_g[l], w_up[l].astype(BF16), ffn_conv_w[l], ffn_conv_b[l],
                           w_down[l].astype(BF16), final_norm_g)
    return x
```

```python
import functools
import math

import jax
import jax.numpy as jnp
from jax import lax
from jax.experimental import pallas as pl
from jax.experimental.pallas import tpu as pltpu

D_MODEL = 1024
HEAD_DIM = 64
N_Q_HEADS = 8
N_KV_HEADS = 2
D_ATTN = N_Q_HEADS * HEAD_DIM
D_KV = N_KV_HEADS * HEAD_DIM
D_CONV = D_MODEL - D_ATTN
D_QKV = D_ATTN + 2 * D_KV
D_IN = D_QKV + 3 * D_CONV
WINDOW = 128
D_FF = 2816
N_MOD = 6
ROPE_THETA = 10000.0
EPS = 1e-6
MASK_VALUE = -1e30
LOG2_E = math.log2(math.e)

V7X_LANES = 128
V7X_SUBLANES = 8
V7X_MXU_COLS = 256
V7X_VMEM_BYTES = 64 * 1024 * 1024

SEQ_TILE = 1024
FF_CHUNK = V7X_MXU_COLS
PAIR = 2 * HEAD_DIM
PV_LAG = 2

F32 = jnp.float32
BF16 = jnp.bfloat16


def _vmem_limit(resident_bytes):
    return min(2 * resident_bytes, V7X_VMEM_BYTES - 4 * 1024 * 1024)


def _rms_norm(x, g):
    ms = jnp.mean(x * x, axis=-1, keepdims=True)
    return x * lax.rsqrt(ms + EPS) * g


def _causal_conv3(cur, prev, w):
    rows = lax.broadcasted_iota(jnp.int32, (V7X_SUBLANES, cur.shape[1]), 0)
    p1 = prev[V7X_SUBLANES - 1:V7X_SUBLANES]
    p2 = prev[V7X_SUBLANES - 2:V7X_SUBLANES - 1]
    r1 = pltpu.roll(cur, 1, 0)
    r2 = pltpu.roll(cur, 2, 0)
    top1 = jnp.where(rows == 0, p1, r1[0:V7X_SUBLANES])
    top2 = jnp.where(rows == 0, p2, jnp.where(rows == 1, p1, r2[0:V7X_SUBLANES]))
    r1 = jnp.concatenate([top1, r1[V7X_SUBLANES:]], axis=0)
    r2 = jnp.concatenate([top2, r2[V7X_SUBLANES:]], axis=0)
    return r2 * w[0:1] + r1 * w[1:2] + cur * w[2:3]


def _mod_kernel(c_ref, w_ref, b_ref, o_ref):
    c_act = jax.nn.silu(c_ref[...])
    o_ref[...] = jnp.dot(c_act.astype(BF16), w_ref[...].astype(BF16),
                         preferred_element_type=F32) + b_ref[...]


def _modulation(c, ada_w, ada_b):
    bsz = c.shape[0]
    n = ada_w.shape[1]
    blk = D_MODEL
    return pl.pallas_call(
        _mod_kernel,
        grid=(n // blk,),
        in_specs=[
            pl.BlockSpec((bsz, D_MODEL), lambda j: (0, 0)),
            pl.BlockSpec((D_MODEL, blk), lambda j: (0, j)),
            pl.BlockSpec((1, blk), lambda j: (0, j)),
        ],
        out_specs=pl.BlockSpec((bsz, blk), lambda j: (0, j)),
        out_shape=jax.ShapeDtypeStruct((bsz, n), F32),
        name="adaln_modulation",
    )(c, ada_w, ada_b.reshape(1, n))


def _mixer_kernel(sinks_ref, x_ref, pos_ref, mod_ref, invf_ref, n1g_ref, win_ref, bin_ref,
                  cw_ref, ga_ref, gc_ref, wo_ref, o_ref,
                  q_s, k_s, v_s, ya_s, cc_s, s_s, m_s, p_s, t_s, y_s):
    t = pl.program_id(1)
    tile = x_ref.shape[1]
    nblk = tile // WINDOW

    lane = lax.broadcasted_iota(jnp.int32, (1, PAIR), 1)
    low_head = lane < HEAD_DIM
    first_half = (lane % HEAD_DIM) < (HEAD_DIM // 2)

    @pl.when(t > 0)
    def _():
        k_s[:, 0:WINDOW, :] = k_s[:, tile:tile + WINDOW, :]
        v_s[:, 0:WINDOW, 0:PAIR] = v_s[:, tile:tile + WINDOW, 0:PAIR]

    @pl.when(t == 0)
    def _():
        k_s[:, 0:WINDOW, :] = jnp.zeros((N_KV_HEADS, WINDOW, PAIR), BF16)
        v_s[:, 0:WINDOW, 0:PAIR] = jnp.zeros((2 * N_KV_HEADS, WINDOW, PAIR), BF16)
        ones_low = jnp.broadcast_to(jnp.where(low_head, 1.0, 0.0).astype(BF16), (tile + WINDOW, PAIR))
        ones_high = jnp.broadcast_to(jnp.where(low_head, 0.0, 1.0).astype(BF16), (tile + WINDOW, PAIR))
        for g in range(N_KV_HEADS):
            v_s[2 * g, :, PAIR:] = ones_low
            v_s[2 * g + 1, :, PAIR:] = ones_high
        cc_s[...] = jnp.zeros(cc_s.shape, F32)

    x = x_ref[0]
    mod = mod_ref[0]
    h = _rms_norm(x, n1g_ref[...] * (1.0 + mod[1:2])) + mod[0:1]
    hb = h.astype(BF16)

    qkv = jnp.dot(hb, win_ref[:, 0:D_QKV], preferred_element_type=F32) + bin_ref[:, 0:D_QKV]

    ang = invf_ref[...] * pos_ref[0].astype(F32)
    cos_f, sin_f = jnp.cos(ang), jnp.sin(ang)
    cos_t = jnp.concatenate([cos_f, cos_f, cos_f, cos_f], axis=0).T
    sin_t = jnp.concatenate([-sin_f, sin_f, -sin_f, sin_f], axis=0).T

    def rope(u):
        rot = jnp.where(first_half, pltpu.roll(u, PAIR - HEAD_DIM // 2, 1),
                        pltpu.roll(u, HEAD_DIM // 2, 1))
        return u * cos_t + rot * sin_t

    scale = LOG2_E / math.sqrt(HEAD_DIM)
    for p in range(D_ATTN // PAIR):
        q_s[:, p * PAIR:(p + 1) * PAIR] = (rope(qkv[:, p * PAIR:(p + 1) * PAIR]) * scale).astype(BF16)

    k = rope(qkv[:, D_ATTN:D_ATTN + D_KV])
    k_sw = pltpu.roll(k, HEAD_DIM, 1)
    v = qkv[:, D_ATTN + D_KV:D_QKV]
    v_sw = pltpu.roll(v, HEAD_DIM, 1)
    cur = pl.ds(WINDOW, tile)
    k_s[0, cur, :] = jnp.where(low_head, k, k_sw).astype(BF16)
    k_s[1, cur, :] = jnp.where(low_head, k_sw, k).astype(BF16)
    v_s[0, cur, 0:PAIR] = jnp.where(low_head, v, 0.0).astype(BF16)
    v_s[1, cur, 0:PAIR] = jnp.where(low_head, 0.0, v_sw).astype(BF16)
    v_s[2, cur, 0:PAIR] = jnp.where(low_head, v_sw, 0.0).astype(BF16)
    v_s[3, cur, 0:PAIR] = jnp.where(low_head, 0.0, v).astype(BF16)

    cproj = jnp.dot(hb, win_ref[:, D_QKV:D_IN], preferred_element_type=F32) + bin_ref[:, D_QKV:D_IN]
    bx = cproj[:, 0:D_CONV] * cproj[:, 2 * D_CONV:3 * D_CONV]
    y_conv = cproj[:, D_CONV:2 * D_CONV] * _causal_conv3(bx, cc_s[...], cw_ref[...])
    cc_s[...] = bx[tile - V7X_SUBLANES:tile]
    y_s[:, D_ATTN:] = _rms_norm(y_conv, gc_ref[...]).astype(BF16)

    kj = lax.broadcasted_iota(jnp.int32, (2 * WINDOW, WINDOW), 0)
    qi = lax.broadcasted_iota(jnp.int32, (2 * WINDOW, WINDOW), 1)
    diff = qi + WINDOW - kj
    band = (diff >= 0) & (diff < WINDOW)
    bias = jnp.where(band, 0.0, MASK_VALUE).astype(BF16)
    bias_first = jnp.where(band & ((kj >= WINDOW) | (t > 0)), 0.0, MASK_VALUE).astype(BF16)
    one_hot = (lax.broadcasted_iota(jnp.int32, (N_Q_HEADS // N_KV_HEADS * WINDOW, WINDOW), 0) % WINDOW
               == lax.broadcasted_iota(jnp.int32, (N_Q_HEADS // N_KV_HEADS * WINDOW, WINDOW), 1)).astype(BF16)
    group = N_Q_HEADS // N_KV_HEADS

    units = [(i, g) for i in range(nblk) for g in range(N_KV_HEADS)]

    for u, (i, g) in enumerate(units):
        pieces = []
        for p in range(g * group // 2, (g + 1) * group // 2):
            qp = q_s[pl.ds(i * WINDOW, WINDOW), p * PAIR:(p + 1) * PAIR]
            zero = jnp.zeros_like(qp)
            pieces += [jnp.where(low_head, qp, zero), jnp.where(low_head, zero, qp)]
        lhs = jnp.concatenate([jnp.concatenate(pieces, axis=0), one_hot], axis=1)
        rhs = jnp.concatenate([k_s[g, pl.ds(i * WINDOW, 2 * WINDOW), :],
                               bias_first if i == 0 else bias], axis=1)
        s = lax.dot_general(lhs, rhs, (((1,), (1,)), ((), ())), preferred_element_type=F32)
        for j in range(group):
            sj = s[j * WINDOW:(j + 1) * WINDOW]
            s_s[u * group + j] = sj
            m = jnp.maximum(jnp.max(sj, axis=-1, keepdims=True), sinks_ref[g * group + j] * LOG2_E)
            m_s[u * group + j] = jnp.broadcast_to(m, (WINDOW, PAIR))

    def exponentials(u, i, g):
        for j in range(group):
            m = m_s[u * group + j]
            e = jnp.exp2(s_s[u * group + j] - jnp.concatenate([m, m], axis=1))
            p_s[u, pl.ds((j // 2) * WINDOW, WINDOW), pl.ds((j % 2) * 2 * WINDOW, 2 * WINDOW)] = e.astype(BF16)
        for jj in range(group // 2):
            sink_lo = sinks_ref[g * group + 2 * jj] * LOG2_E - m_s[u * group + 2 * jj]
            sink_hi = sinks_ref[g * group + 2 * jj + 1] * LOG2_E - m_s[u * group + 2 * jj + 1]
            t_s[u * (group // 2) + jj] = jnp.exp2(jnp.where(low_head, sink_lo, sink_hi))

    def weighted_values(u, i, g):
        krows = pl.ds(i * WINDOW, 2 * WINDOW)
        rhs = jnp.concatenate([v_s[2 * g, krows, :], v_s[2 * g + 1, krows, :]], axis=0)
        o = jnp.dot(p_s[u], rhs, preferred_element_type=F32)
        for jj in range(group // 2):
            oj = o[jj * WINDOW:(jj + 1) * WINDOW]
            den = oj[:, PAIR:] + t_s[u * (group // 2) + jj]
            p = g * group // 2 + jj
            ya_s[pl.ds(i * WINDOW, WINDOW), p * PAIR:(p + 1) * PAIR] = oj[:, 0:PAIR] / den

    for u in range(len(units) + PV_LAG):
        if u < len(units):
            exponentials(u, *units[u])
        if u >= PV_LAG:
            weighted_values(u - PV_LAG, *units[u - PV_LAG])

    y_s[:, 0:D_ATTN] = _rms_norm(ya_s[...], ga_ref[...]).astype(BF16)
    o_ref[0] = x_ref[0] + mod[2:3] * jnp.dot(y_s[...], wo_ref[...], preferred_element_type=F32)


def _token_mixer(x, positions, mod, norm1_g, w_in, b_in, conv_w, sinks, ga, gc, w_o):
    bsz, seq, _ = x.shape
    tile = SEQ_TILE
    assert seq % tile == 0 and tile % WINDOW == 0
    inv_freq = ROPE_THETA ** (-jnp.arange(0, HEAD_DIM // 2, dtype=F32) / (HEAD_DIM // 2))
    const = lambda shape: pl.BlockSpec(shape, lambda b, t: (0,) * len(shape),
                                       pipeline_mode=pl.Buffered(1))
    nhb = (tile // WINDOW) * N_Q_HEADS
    resident = (2 * 2 * tile * D_MODEL * 4 + (D_MODEL * D_IN + D_MODEL * D_MODEL) * 2
                + tile * D_IN * 4 + tile * (D_ATTN * 6 + 6 * PAIR * 2)
                + nhb * WINDOW * WINDOW * (8 + 4 + 2 + 2))
    return pl.pallas_call(
        _mixer_kernel,
        grid=(bsz, seq // tile),
        in_specs=[
            pl.BlockSpec(memory_space=pltpu.SMEM),
            pl.BlockSpec((1, tile, D_MODEL), lambda b, t: (b, t, 0)),
            pl.BlockSpec((1, 1, tile), lambda b, t: (b, 0, t)),
            pl.BlockSpec((1, N_MOD, D_MODEL), lambda b, t: (b, 0, 0)),
            const((HEAD_DIM // 2, 1)),
            const((1, D_MODEL)),
            const((D_MODEL, D_IN)),
            const((1, D_IN)),
            const((3, D_CONV)),
            const((1, D_ATTN)),
            const((1, D_CONV)),
            const((D_MODEL, D_MODEL)),
        ],
        out_specs=pl.BlockSpec((1, tile, D_MODEL), lambda b, t: (b, t, 0)),
        out_shape=jax.ShapeDtypeStruct(x.shape, F32),
        scratch_shapes=[
            pltpu.VMEM((tile, D_ATTN), BF16),
            pltpu.VMEM((N_KV_HEADS, tile + WINDOW, PAIR), BF16),
            pltpu.VMEM((2 * N_KV_HEADS, tile + WINDOW, 2 * PAIR), BF16),
            pltpu.VMEM((tile, D_ATTN), F32),
            pltpu.VMEM((V7X_SUBLANES, D_CONV), F32),
            pltpu.VMEM((nhb, WINDOW, 2 * WINDOW), F32),
            pltpu.VMEM((nhb, WINDOW, PAIR), F32),
            pltpu.VMEM((nhb // 4, 2 * WINDOW, 4 * WINDOW), BF16),
            pltpu.VMEM((nhb // 2, WINDOW, PAIR), F32),
            pltpu.VMEM((tile, D_MODEL), BF16),
        ],
        compiler_params=pltpu.CompilerParams(
            dimension_semantics=("arbitrary", "arbitrary"),
            vmem_limit_bytes=_vmem_limit(resident)),
        name="token_mixer",
    )(sinks, x, positions.reshape(bsz, 1, seq), mod, inv_freq.reshape(-1, 1),
      norm1_g.reshape(1, -1), w_in, b_in.reshape(1, -1), conv_w, ga.reshape(1, -1),
      gc.reshape(1, -1), w_o)


def _ffn_kernel(x_ref, mod_ref, n2g_ref, wup_ref, cw_ref, cb_ref, wdn_ref, fg_ref, o_ref,
                f_s, gc_s):
    t = pl.program_id(1)
    tile = x_ref.shape[1]

    @pl.when(t == 0)
    def _():
        gc_s[...] = jnp.zeros(gc_s.shape, F32)

    x = x_ref[0]
    mod = mod_ref[0]
    h = _rms_norm(x, n2g_ref[...] * (1.0 + mod[4:5])) + mod[3:4]
    hb = h.astype(BF16)

    for c in range(D_FF // FF_CHUNK):
        cols = pl.ds(c * FF_CHUNK, FF_CHUNK)
        gate = jnp.dot(hb, wup_ref[:, cols], preferred_element_type=F32)
        val = jnp.dot(hb, wup_ref[:, pl.ds(D_FF + c * FF_CHUNK, FF_CHUNK)], preferred_element_type=F32)
        conv = _causal_conv3(gate, gc_s[:, cols], cw_ref[:, cols]) + cb_ref[:, cols]
        gc_s[:, cols] = gate[tile - V7X_SUBLANES:tile]
        f_s[:, cols] = (jax.nn.silu(conv) * val).astype(BF16)

    x2 = x + mod[5:6] * jnp.dot(f_s[...], wdn_ref[...], preferred_element_type=F32)
    o_ref[0] = _rms_norm(x2, fg_ref[...])


def _channel_mixer(x, mod, norm2_g, w_up, conv_w, conv_b, w_down, final_g):
    bsz, seq, _ = x.shape
    tile = SEQ_TILE
    assert seq % tile == 0 and D_FF % FF_CHUNK == 0
    const = lambda shape: pl.BlockSpec(shape, lambda b, t: (0,) * len(shape),
                                       pipeline_mode=pl.Buffered(1))
    resident = (2 * 2 * tile * D_MODEL * 4 + 3 * D_MODEL * D_FF * 2 + tile * D_FF * 2
                + 4 * tile * FF_CHUNK * 4)
    return pl.pallas_call(
        _ffn_kernel,
        grid=(bsz, seq // tile),
        in_specs=[
            pl.BlockSpec((1, tile, D_MODEL), lambda b, t: (b, t, 0)),
            pl.BlockSpec((1, N_MOD, D_MODEL), lambda b, t: (b, 0, 0)),
            const((1, D_MODEL)),
            const((D_MODEL, 2 * D_FF)),
            const((3, D_FF)),
            const((1, D_FF)),
            const((D_FF, D_MODEL)),
            const((1, D_MODEL)),
        ],
        out_specs=pl.BlockSpec((1, tile, D_MODEL), lambda b, t: (b, t, 0)),
        out_shape=jax.ShapeDtypeStruct(x.shape, F32),
        scratch_shapes=[
            pltpu.VMEM((tile, D_FF), BF16),
            pltpu.VMEM((V7X_SUBLANES, D_FF), F32),
        ],
        compiler_params=pltpu.CompilerParams(
            dimension_semantics=("arbitrary", "arbitrary"),
            vmem_limit_bytes=_vmem_limit(resident)),
        name="channel_mixer",
    )(x, mod, norm2_g.reshape(1, -1), w_up, conv_w, conv_b.reshape(1, -1), w_down,
      final_g.reshape(1, -1))


def kernel(x, c, positions, ada_w, ada_b, norm1_g, w_in, b_in, conv_w, attn_sinks, out_norm_attn_g, out_norm_conv_g, w_o, norm2_g, w_up, ffn_conv_w, ffn_conv_b, w_down, final_norm_g):
    depth = ada_w.shape[0]
    bsz = x.shape[0]
    for l in range(depth):
        mod = _modulation(c, ada_w[l], ada_b[l]).reshape(bsz, N_MOD, D_MODEL)
        x = _token_mixer(x, positions, mod, norm1_g[l], w_in[l].astype(BF16), b_in[l], conv_w[l],
                         attn_sinks[l], out_norm_attn_g[l], out_norm_conv_g[l], w_o[l].astype(BF16))
        last = l == depth - 1
        assert last, "the final RMSNorm is fused into the last layer's channel mixer"
        x = _channel_mixer(x, mod, norm2_g[l], w_up[l].astype(BF16), ffn_conv_w[l], ffn_conv_b[l],
                           w_down[l].astype(BF16), final_norm_g)
    return x
```

```python
import functools
import math

import jax
import jax.numpy as jnp
from jax import lax
from jax.experimental import pallas as pl
from jax.experimental.pallas import tpu as pltpu

D_MODEL = 1024
HEAD_DIM = 64
N_Q_HEADS = 8
N_KV_HEADS = 2
D_ATTN = N_Q_HEADS * HEAD_DIM
D_KV = N_KV_HEADS * HEAD_DIM
D_CONV = D_MODEL - D_ATTN
D_QKV = D_ATTN + 2 * D_KV
D_IN = D_QKV + 3 * D_CONV
WINDOW = 128
D_FF = 2816
N_MOD = 6
ROPE_THETA = 10000.0
EPS = 1e-6
MASK_VALUE = -1e30
LOG2_E = math.log2(math.e)

V7X_LANES = 128
V7X_SUBLANES = 8
V7X_MXU_COLS = 256
V7X_VMEM_BYTES = 64 * 1024 * 1024

SEQ_TILE = 1024
FF_CHUNK = V7X_MXU_COLS
PAIR = 2 * HEAD_DIM
PV_LAG = 2
HEAD_ROW_BLOCKS = 2
TAIL_ROW_BLOCKS = 4

F32 = jnp.float32
BF16 = jnp.bfloat16


def _vmem_limit(resident_bytes):
    return min(2 * resident_bytes, V7X_VMEM_BYTES - 4 * 1024 * 1024)


def _rms_norm(x, g):
    ms = jnp.mean(x * x, axis=-1, keepdims=True)
    return x * lax.rsqrt(ms + EPS) * g


def _causal_conv3(cur, prev, w):
    rows = lax.broadcasted_iota(jnp.int32, (V7X_SUBLANES, cur.shape[1]), 0)
    p1 = prev[V7X_SUBLANES - 1:V7X_SUBLANES]
    p2 = prev[V7X_SUBLANES - 2:V7X_SUBLANES - 1]
    r1 = pltpu.roll(cur, 1, 0)
    r2 = pltpu.roll(cur, 2, 0)
    top1 = jnp.where(rows == 0, p1, r1[0:V7X_SUBLANES])
    top2 = jnp.where(rows == 0, p2, jnp.where(rows == 1, p1, r2[0:V7X_SUBLANES]))
    r1 = jnp.concatenate([top1, r1[V7X_SUBLANES:]], axis=0)
    r2 = jnp.concatenate([top2, r2[V7X_SUBLANES:]], axis=0)
    return r2 * w[0:1] + r1 * w[1:2] + cur * w[2:3]


def _mod_kernel(c_ref, w_ref, b_ref, o_ref):
    c_act = jax.nn.silu(c_ref[...])
    o_ref[...] = jnp.dot(c_act.astype(BF16), w_ref[...].astype(BF16),
                         preferred_element_type=F32) + b_ref[...]


def _modulation(c, ada_w, ada_b):
    bsz = c.shape[0]
    n = ada_w.shape[1]
    blk = D_MODEL
    return pl.pallas_call(
        _mod_kernel,
        grid=(n // blk,),
        in_specs=[
            pl.BlockSpec((bsz, D_MODEL), lambda j: (0, 0)),
            pl.BlockSpec((D_MODEL, blk), lambda j: (0, j)),
            pl.BlockSpec((1, blk), lambda j: (0, j)),
        ],
        out_specs=pl.BlockSpec((bsz, blk), lambda j: (0, j)),
        out_shape=jax.ShapeDtypeStruct((bsz, n), F32),
        name="adaln_modulation",
    )(c, ada_w, ada_b.reshape(1, n))


def _mixer_kernel(sinks_ref, x_ref, pos_ref, mod_ref, invf_ref, n1g_ref, win_ref, bin_ref,
                  cw_ref, ga_ref, gc_ref, wo_ref, o_ref,
                  q_s, k_s, v_s, ya_s, cc_s, s_s, m_s, p_s, t_s, y_s):
    t = pl.program_id(1)
    tile = x_ref.shape[1]
    nblk = tile // WINDOW

    lane = lax.broadcasted_iota(jnp.int32, (1, PAIR), 1)
    low_head = lane < HEAD_DIM
    first_half = (lane % HEAD_DIM) < (HEAD_DIM // 2)

    @pl.when(t > 0)
    def _():
        k_s[:, 0:WINDOW, :] = k_s[:, tile:tile + WINDOW, :]
        v_s[:, 0:WINDOW, 0:PAIR] = v_s[:, tile:tile + WINDOW, 0:PAIR]

    @pl.when(t == 0)
    def _():
        k_s[:, 0:WINDOW, :] = jnp.zeros((N_KV_HEADS, WINDOW, PAIR), BF16)
        v_s[:, 0:WINDOW, 0:PAIR] = jnp.zeros((2 * N_KV_HEADS, WINDOW, PAIR), BF16)
        ones_low = jnp.broadcast_to(jnp.where(low_head, 1.0, 0.0).astype(BF16), (tile + WINDOW, PAIR))
        ones_high = jnp.broadcast_to(jnp.where(low_head, 0.0, 1.0).astype(BF16), (tile + WINDOW, PAIR))
        for g in range(N_KV_HEADS):
            v_s[2 * g, :, PAIR:] = ones_low
            v_s[2 * g + 1, :, PAIR:] = ones_high
        cc_s[...] = jnp.zeros(cc_s.shape, F32)

    x = x_ref[0]
    mod = mod_ref[0]
    h = _rms_norm(x, n1g_ref[...] * (1.0 + mod[1:2])) + mod[0:1]
    hb = h.astype(BF16)

    qkv = jnp.dot(hb, win_ref[:, 0:D_QKV], preferred_element_type=F32) + bin_ref[:, 0:D_QKV]

    ang = invf_ref[...] * pos_ref[0].astype(F32)
    cos_f, sin_f = jnp.cos(ang), jnp.sin(ang)
    cos_t = jnp.concatenate([cos_f, cos_f, cos_f, cos_f], axis=0).T
    sin_t = jnp.concatenate([-sin_f, sin_f, -sin_f, sin_f], axis=0).T

    def rope(u):
        rot = jnp.where(first_half, pltpu.roll(u, PAIR - HEAD_DIM // 2, 1),
                        pltpu.roll(u, HEAD_DIM // 2, 1))
        return u * cos_t + rot * sin_t

    scale = LOG2_E / math.sqrt(HEAD_DIM)
    for p in range(D_ATTN // PAIR):
        q_s[:, p * PAIR:(p + 1) * PAIR] = (rope(qkv[:, p * PAIR:(p + 1) * PAIR]) * scale).astype(BF16)

    k = rope(qkv[:, D_ATTN:D_ATTN + D_KV])
    k_sw = pltpu.roll(k, HEAD_DIM, 1)
    v = qkv[:, D_ATTN + D_KV:D_QKV]
    v_sw = pltpu.roll(v, HEAD_DIM, 1)
    cur = pl.ds(WINDOW, tile)
    k_s[0, cur, :] = jnp.where(low_head, k, k_sw).astype(BF16)
    k_s[1, cur, :] = jnp.where(low_head, k_sw, k).astype(BF16)
    v_s[0, cur, 0:PAIR] = jnp.where(low_head, v, 0.0).astype(BF16)
    v_s[1, cur, 0:PAIR] = jnp.where(low_head, 0.0, v_sw).astype(BF16)
    v_s[2, cur, 0:PAIR] = jnp.where(low_head, v_sw, 0.0).astype(BF16)
    v_s[3, cur, 0:PAIR] = jnp.where(low_head, 0.0, v).astype(BF16)

    cproj = jnp.dot(hb, win_ref[:, D_QKV:D_IN], preferred_element_type=F32) + bin_ref[:, D_QKV:D_IN]
    bx = cproj[:, 0:D_CONV] * cproj[:, 2 * D_CONV:3 * D_CONV]
    y_conv = cproj[:, D_CONV:2 * D_CONV] * _causal_conv3(bx, cc_s[...], cw_ref[...])
    cc_s[...] = bx[tile - V7X_SUBLANES:tile]
    y_s[:, D_ATTN:] = _rms_norm(y_conv, gc_ref[...]).astype(BF16)

    kj = lax.broadcasted_iota(jnp.int32, (2 * WINDOW, WINDOW), 0)
    qi = lax.broadcasted_iota(jnp.int32, (2 * WINDOW, WINDOW), 1)
    diff = qi + WINDOW - kj
    band = (diff >= 0) & (diff < WINDOW)
    bias = jnp.where(band, 0.0, MASK_VALUE).astype(BF16)
    bias_first = jnp.where(band & ((kj >= WINDOW) | (t > 0)), 0.0, MASK_VALUE).astype(BF16)
    one_hot = (lax.broadcasted_iota(jnp.int32, (N_Q_HEADS // N_KV_HEADS * WINDOW, WINDOW), 0) % WINDOW
               == lax.broadcasted_iota(jnp.int32, (N_Q_HEADS // N_KV_HEADS * WINDOW, WINDOW), 1)).astype(BF16)
    group = N_Q_HEADS // N_KV_HEADS

    units = [(i, g) for i in range(nblk) for g in range(N_KV_HEADS)]

    for u, (i, g) in enumerate(units):
        pieces = []
        for p in range(g * group // 2, (g + 1) * group // 2):
            qp = q_s[pl.ds(i * WINDOW, WINDOW), p * PAIR:(p + 1) * PAIR]
            zero = jnp.zeros_like(qp)
            pieces += [jnp.where(low_head, qp, zero), jnp.where(low_head, zero, qp)]
        lhs = jnp.concatenate([jnp.concatenate(pieces, axis=0), one_hot], axis=1)
        rhs = jnp.concatenate([k_s[g, pl.ds(i * WINDOW, 2 * WINDOW), :],
                               bias_first if i == 0 else bias], axis=1)
        s = lax.dot_general(lhs, rhs, (((1,), (1,)), ((), ())), preferred_element_type=F32)
        for j in range(group):
            sj = s[j * WINDOW:(j + 1) * WINDOW]
            s_s[u * group + j] = sj
            m = jnp.maximum(jnp.max(sj, axis=-1, keepdims=True), sinks_ref[g * group + j] * LOG2_E)
            m_s[u * group + j] = jnp.broadcast_to(m, (WINDOW, PAIR))

    def exponentials(u, i, g):
        for j in range(group):
            m = m_s[u * group + j]
            e = jnp.exp2(s_s[u * group + j] - jnp.concatenate([m, m], axis=1))
            p_s[u, pl.ds((j // 2) * WINDOW, WINDOW), pl.ds((j % 2) * 2 * WINDOW, 2 * WINDOW)] = e.astype(BF16)
        for jj in range(group // 2):
            sink_lo = sinks_ref[g * group + 2 * jj] * LOG2_E - m_s[u * group + 2 * jj]
            sink_hi = sinks_ref[g * group + 2 * jj + 1] * LOG2_E - m_s[u * group + 2 * jj + 1]
            t_s[u * (group // 2) + jj] = jnp.exp2(jnp.where(low_head, sink_lo, sink_hi))

    def weighted_values(u, i, g):
        krows = pl.ds(i * WINDOW, 2 * WINDOW)
        rhs = jnp.concatenate([v_s[2 * g, krows, :], v_s[2 * g + 1, krows, :]], axis=0)
        o = jnp.dot(p_s[u], rhs, preferred_element_type=F32)
        for jj in range(group // 2):
            oj = o[jj * WINDOW:(jj + 1) * WINDOW]
            den = oj[:, PAIR:] + t_s[u * (group // 2) + jj]
            p = g * group // 2 + jj
            ya_s[pl.ds(i * WINDOW, WINDOW), p * PAIR:(p + 1) * PAIR] = oj[:, 0:PAIR] / den

    for u in range(len(units) + PV_LAG):
        if u < len(units):
            exponentials(u, *units[u])
        if u >= PV_LAG:
            weighted_values(u - PV_LAG, *units[u - PV_LAG])

    y_s[:, 0:D_ATTN] = _rms_norm(ya_s[...], ga_ref[...]).astype(BF16)
    o_ref[0] = x_ref[0] + mod[2:3] * jnp.dot(y_s[...], wo_ref[...], preferred_element_type=F32)


def _token_mixer(x, positions, mod, norm1_g, w_in, b_in, conv_w, sinks, ga, gc, w_o):
    bsz, seq, _ = x.shape
    tile = SEQ_TILE
    assert seq % tile == 0 and tile % WINDOW == 0
    inv_freq = ROPE_THETA ** (-jnp.arange(0, HEAD_DIM // 2, dtype=F32) / (HEAD_DIM // 2))
    const = lambda shape: pl.BlockSpec(shape, lambda b, t: (0,) * len(shape),
                                       pipeline_mode=pl.Buffered(1))
    nhb = (tile // WINDOW) * N_Q_HEADS
    resident = (2 * 2 * tile * D_MODEL * 4 + (D_MODEL * D_IN + D_MODEL * D_MODEL) * 2
                + tile * D_IN * 4 + tile * (D_ATTN * 6 + 6 * PAIR * 2)
                + nhb * WINDOW * WINDOW * (8 + 4 + 2 + 2))
    return pl.pallas_call(
        _mixer_kernel,
        grid=(bsz, seq // tile),
        in_specs=[
            pl.BlockSpec(memory_space=pltpu.SMEM),
            pl.BlockSpec((1, tile, D_MODEL), lambda b, t: (b, t, 0)),
            pl.BlockSpec((1, 1, tile), lambda b, t: (b, 0, t)),
            pl.BlockSpec((1, N_MOD, D_MODEL), lambda b, t: (b, 0, 0)),
            const((HEAD_DIM // 2, 1)),
            const((1, D_MODEL)),
            const((D_MODEL, D_IN)),
            const((1, D_IN)),
            const((3, D_CONV)),
            const((1, D_ATTN)),
            const((1, D_CONV)),
            const((D_MODEL, D_MODEL)),
        ],
        out_specs=pl.BlockSpec((1, tile, D_MODEL), lambda b, t: (b, t, 0)),
        out_shape=jax.ShapeDtypeStruct(x.shape, F32),
        scratch_shapes=[
            pltpu.VMEM((tile, D_ATTN), BF16),
            pltpu.VMEM((N_KV_HEADS, tile + WINDOW, PAIR), BF16),
            pltpu.VMEM((2 * N_KV_HEADS, tile + WINDOW, 2 * PAIR), BF16),
            pltpu.VMEM((tile, D_ATTN), F32),
            pltpu.VMEM((V7X_SUBLANES, D_CONV), F32),
            pltpu.VMEM((nhb, WINDOW, 2 * WINDOW), F32),
            pltpu.VMEM((nhb, WINDOW, PAIR), F32),
            pltpu.VMEM((nhb // 4, 2 * WINDOW, 4 * WINDOW), BF16),
            pltpu.VMEM((nhb // 2, WINDOW, PAIR), F32),
            pltpu.VMEM((tile, D_MODEL), BF16),
        ],
        compiler_params=pltpu.CompilerParams(
            dimension_semantics=("arbitrary", "arbitrary"),
            vmem_limit_bytes=_vmem_limit(resident)),
        name="token_mixer",
    )(sinks, x, positions.reshape(bsz, 1, seq), mod, inv_freq.reshape(-1, 1),
      norm1_g.reshape(1, -1), w_in, b_in.reshape(1, -1), conv_w, ga.reshape(1, -1),
      gc.reshape(1, -1), w_o)


def _ffn_kernel(x_ref, mod_ref, n2g_ref, wup_ref, cw_ref, cb_ref, wdn_ref, fg_ref, o_ref,
                f_s, gc_s):
    t = pl.program_id(1)
    tile = x_ref.shape[1]

    @pl.when(t == 0)
    def _():
        gc_s[...] = jnp.zeros(gc_s.shape, F32)

    x = x_ref[0]
    mod = mod_ref[0]
    h = _rms_norm(x, n2g_ref[...] * (1.0 + mod[4:5])) + mod[3:4]
    hb = h.astype(BF16)

    for c in range(D_FF // FF_CHUNK):
        cols = pl.ds(c * FF_CHUNK, FF_CHUNK)
        vcols = pl.ds(D_FF + c * FF_CHUNK, FF_CHUNK)
        if c == 0:
            rb = tile // HEAD_ROW_BLOCKS
            blocks = [hb[r * rb:(r + 1) * rb] for r in range(HEAD_ROW_BLOCKS)]
            gate = jnp.concatenate([jnp.dot(b, wup_ref[:, cols], preferred_element_type=F32)
                                    for b in blocks], axis=0)
            val = jnp.concatenate([jnp.dot(b, wup_ref[:, vcols], preferred_element_type=F32)
                                   for b in blocks], axis=0)
        else:
            gate = jnp.dot(hb, wup_ref[:, cols], preferred_element_type=F32)
            val = jnp.dot(hb, wup_ref[:, vcols], preferred_element_type=F32)
        conv = _causal_conv3(gate, gc_s[:, cols], cw_ref[:, cols]) + cb_ref[:, cols]
        gc_s[:, cols] = gate[tile - V7X_SUBLANES:tile]
        f_s[:, cols] = (jax.nn.silu(conv) * val).astype(BF16)

    rb = tile // TAIL_ROW_BLOCKS
    for r in range(TAIL_ROW_BLOCKS):
        rows = pl.ds(r * rb, rb)
        x2 = x_ref[0, rows, :] + mod[5:6] * jnp.dot(f_s[rows, :], wdn_ref[...], preferred_element_type=F32)
        o_ref[0, rows, :] = _rms_norm(x2, fg_ref[...])


def _channel_mixer(x, mod, norm2_g, w_up, conv_w, conv_b, w_down, final_g):
    bsz, seq, _ = x.shape
    tile = SEQ_TILE
    assert seq % tile == 0 and D_FF % FF_CHUNK == 0
    const = lambda shape: pl.BlockSpec(shape, lambda b, t: (0,) * len(shape),
                                       pipeline_mode=pl.Buffered(1))
    resident = (2 * 2 * tile * D_MODEL * 4 + 3 * D_MODEL * D_FF * 2 + tile * D_FF * 2
                + 4 * tile * FF_CHUNK * 4)
    return pl.pallas_call(
        _ffn_kernel,
        grid=(bsz, seq // tile),
        in_specs=[
            pl.BlockSpec((1, tile, D_MODEL), lambda b, t: (b, t, 0)),
            pl.BlockSpec((1, N_MOD, D_MODEL), lambda b, t: (b, 0, 0)),
            const((1, D_MODEL)),
            const((D_MODEL, 2 * D_FF)),
            const((3, D_FF)),
            const((1, D_FF)),
            const((D_FF, D_MODEL)),
            const((1, D_MODEL)),
        ],
        out_specs=pl.BlockSpec((1, tile, D_MODEL), lambda b, t: (b, t, 0)),
        out_shape=jax.ShapeDtypeStruct(x.shape, F32),
        scratch_shapes=[
            pltpu.VMEM((tile, D_FF), BF16),
            pltpu.VMEM((V7X_SUBLANES, D_FF), F32),
        ],
        compiler_params=pltpu.CompilerParams(
            dimension_semantics=("arbitrary", "arbitrary"),
            vmem_limit_bytes=_vmem_limit(resident)),
        name="channel_mixer",
    )(x, mod, norm2_g.reshape(1, -1), w_up, conv_w, conv_b.reshape(1, -1), w_down,
      final_g.reshape(1, -1))


def kernel(x, c, positions, ada_w, ada_b, norm1_g, w_in, b_in, conv_w, attn_sinks, out_norm_attn_g, out_norm_conv_g, w_o, norm2_g, w_up, ffn_conv_w, ffn_conv_b, w_down, final_norm_g):
    depth = ada_w.shape[0]
    bsz = x.shape[0]
    for l in range(depth):
        mod = _modulation(c, ada_w[l], ada_b[l]).reshape(bsz, N_MOD, D_MODEL)
        x = _token_mixer(x, positions, mod, norm1_g[l], w_in[l].astype(BF16), b_in[l], conv_w[l],
                         attn_sinks[l], out_norm_attn_g[l], out_norm_conv_g[l], w_o[l].astype(BF16))
        last = l == depth - 1
        assert last, "the final RMSNorm is fused into the last layer's channel mixer"
        x = _channel_mixer(x, mod, norm2_g[l], w_up[l].astype(BF16), ffn_conv_w[l], ffn_conv_b[l],
                           w_down[l].astype(BF16), final_norm_g)
    return x
```

```python
import functools
import math

import jax
import jax.numpy as jnp
from jax import lax
from jax.experimental import pallas as pl
from jax.experimental.pallas import tpu as pltpu

D_MODEL = 1024
HEAD_DIM = 64
N_Q_HEADS = 8
N_KV_HEADS = 2
D_ATTN = N_Q_HEADS * HEAD_DIM
D_KV = N_KV_HEADS * HEAD_DIM
D_CONV = D_MODEL - D_ATTN
D_QKV = D_ATTN + 2 * D_KV
D_IN = D_QKV + 3 * D_CONV
WINDOW = 128
D_FF = 2816
N_MOD = 6
ROPE_THETA = 10000.0
EPS = 1e-6
MASK_VALUE = -1e30
LOG2_E = math.log2(math.e)

V7X_LANES = 128
V7X_SUBLANES = 8
V7X_MXU_COLS = 256
V7X_VMEM_BYTES = 64 * 1024 * 1024

SEQ_TILE = 1024
FF_CHUNK = V7X_MXU_COLS
PAIR = 2 * HEAD_DIM
PV_LAG = 2
HEAD_ROWS = 128
TAIL_ROW_BLOCKS = 4

F32 = jnp.float32
BF16 = jnp.bfloat16


def _vmem_limit(resident_bytes):
    return min(2 * resident_bytes, V7X_VMEM_BYTES - 4 * 1024 * 1024)


def _rms_norm(x, g):
    ms = jnp.mean(x * x, axis=-1, keepdims=True)
    return x * lax.rsqrt(ms + EPS) * g


def _causal_conv3(cur, prev, w):
    rows = lax.broadcasted_iota(jnp.int32, (V7X_SUBLANES, cur.shape[1]), 0)
    p1 = prev[V7X_SUBLANES - 1:V7X_SUBLANES]
    p2 = prev[V7X_SUBLANES - 2:V7X_SUBLANES - 1]
    r1 = pltpu.roll(cur, 1, 0)
    r2 = pltpu.roll(cur, 2, 0)
    top1 = jnp.where(rows == 0, p1, r1[0:V7X_SUBLANES])
    top2 = jnp.where(rows == 0, p2, jnp.where(rows == 1, p1, r2[0:V7X_SUBLANES]))
    r1 = jnp.concatenate([top1, r1[V7X_SUBLANES:]], axis=0)
    r2 = jnp.concatenate([top2, r2[V7X_SUBLANES:]], axis=0)
    return r2 * w[0:1] + r1 * w[1:2] + cur * w[2:3]


def _mod_kernel(c_ref, w_ref, b_ref, o_ref):
    c_act = jax.nn.silu(c_ref[...])
    o_ref[...] = jnp.dot(c_act.astype(BF16), w_ref[...].astype(BF16),
                         preferred_element_type=F32) + b_ref[...]


def _modulation(c, ada_w, ada_b):
    bsz = c.shape[0]
    n = ada_w.shape[1]
    blk = D_MODEL
    return pl.pallas_call(
        _mod_kernel,
        grid=(n // blk,),
        in_specs=[
            pl.BlockSpec((bsz, D_MODEL), lambda j: (0, 0)),
            pl.BlockSpec((D_MODEL, blk), lambda j: (0, j)),
            pl.BlockSpec((1, blk), lambda j: (0, j)),
        ],
        out_specs=pl.BlockSpec((bsz, blk), lambda j: (0, j)),
        out_shape=jax.ShapeDtypeStruct((bsz, n), F32),
        name="adaln_modulation",
    )(c, ada_w, ada_b.reshape(1, n))


def _mixer_kernel(sinks_ref, x_ref, pos_ref, mod_ref, invf_ref, n1g_ref, win_ref, bin_ref,
                  cw_ref, ga_ref, gc_ref, wo_ref, o_ref,
                  q_s, k_s, v_s, ya_s, cc_s, s_s, m_s, p_s, t_s, y_s):
    t = pl.program_id(1)
    tile = x_ref.shape[1]
    nblk = tile // WINDOW

    lane = lax.broadcasted_iota(jnp.int32, (1, PAIR), 1)
    low_head = lane < HEAD_DIM
    first_half = (lane % HEAD_DIM) < (HEAD_DIM // 2)

    @pl.when(t > 0)
    def _():
        k_s[:, 0:WINDOW, :] = k_s[:, tile:tile + WINDOW, :]
        v_s[:, 0:WINDOW, 0:PAIR] = v_s[:, tile:tile + WINDOW, 0:PAIR]

    @pl.when(t == 0)
    def _():
        k_s[:, 0:WINDOW, :] = jnp.zeros((N_KV_HEADS, WINDOW, PAIR), BF16)
        v_s[:, 0:WINDOW, 0:PAIR] = jnp.zeros((2 * N_KV_HEADS, WINDOW, PAIR), BF16)
        ones_low = jnp.broadcast_to(jnp.where(low_head, 1.0, 0.0).astype(BF16), (tile + WINDOW, PAIR))
        ones_high = jnp.broadcast_to(jnp.where(low_head, 0.0, 1.0).astype(BF16), (tile + WINDOW, PAIR))
        for g in range(N_KV_HEADS):
            v_s[2 * g, :, PAIR:] = ones_low
            v_s[2 * g + 1, :, PAIR:] = ones_high
        cc_s[...] = jnp.zeros(cc_s.shape, F32)

    x = x_ref[0]
    mod = mod_ref[0]
    h = _rms_norm(x, n1g_ref[...] * (1.0 + mod[1:2])) + mod[0:1]
    hb = h.astype(BF16)

    qkv = jnp.dot(hb, win_ref[:, 0:D_QKV], preferred_element_type=F32) + bin_ref[:, 0:D_QKV]

    ang = invf_ref[...] * pos_ref[0].astype(F32)
    cos_f, sin_f = jnp.cos(ang), jnp.sin(ang)
    cos_t = jnp.concatenate([cos_f, cos_f, cos_f, cos_f], axis=0).T
    sin_t = jnp.concatenate([-sin_f, sin_f, -sin_f, sin_f], axis=0).T

    def rope(u):
        rot = jnp.where(first_half, pltpu.roll(u, PAIR - HEAD_DIM // 2, 1),
                        pltpu.roll(u, HEAD_DIM // 2, 1))
        return u * cos_t + rot * sin_t

    scale = LOG2_E / math.sqrt(HEAD_DIM)
    for p in range(D_ATTN // PAIR):
        q_s[:, p * PAIR:(p + 1) * PAIR] = (rope(qkv[:, p * PAIR:(p + 1) * PAIR]) * scale).astype(BF16)

    k = rope(qkv[:, D_ATTN:D_ATTN + D_KV])
    k_sw = pltpu.roll(k, HEAD_DIM, 1)
    v = qkv[:, D_ATTN + D_KV:D_QKV]
    v_sw = pltpu.roll(v, HEAD_DIM, 1)
    cur = pl.ds(WINDOW, tile)
    k_s[0, cur, :] = jnp.where(low_head, k, k_sw).astype(BF16)
    k_s[1, cur, :] = jnp.where(low_head, k_sw, k).astype(BF16)
    v_s[0, cur, 0:PAIR] = jnp.where(low_head, v, 0.0).astype(BF16)
    v_s[1, cur, 0:PAIR] = jnp.where(low_head, 0.0, v_sw).astype(BF16)
    v_s[2, cur, 0:PAIR] = jnp.where(low_head, v_sw, 0.0).astype(BF16)
    v_s[3, cur, 0:PAIR] = jnp.where(low_head, 0.0, v).astype(BF16)

    cproj = jnp.dot(hb, win_ref[:, D_QKV:D_IN], preferred_element_type=F32) + bin_ref[:, D_QKV:D_IN]
    bx = cproj[:, 0:D_CONV] * cproj[:, 2 * D_CONV:3 * D_CONV]
    y_conv = cproj[:, D_CONV:2 * D_CONV] * _causal_conv3(bx, cc_s[...], cw_ref[...])
    cc_s[...] = bx[tile - V7X_SUBLANES:tile]
    y_s[:, D_ATTN:] = _rms_norm(y_conv, gc_ref[...]).astype(BF16)

    kj = lax.broadcasted_iota(jnp.int32, (2 * WINDOW, WINDOW), 0)
    qi = lax.broadcasted_iota(jnp.int32, (2 * WINDOW, WINDOW), 1)
    diff = qi + WINDOW - kj
    band = (diff >= 0) & (diff < WINDOW)
    bias = jnp.where(band, 0.0, MASK_VALUE).astype(BF16)
    bias_first = jnp.where(band & ((kj >= WINDOW) | (t > 0)), 0.0, MASK_VALUE).astype(BF16)
    one_hot = (lax.broadcasted_iota(jnp.int32, (N_Q_HEADS // N_KV_HEADS * WINDOW, WINDOW), 0) % WINDOW
               == lax.broadcasted_iota(jnp.int32, (N_Q_HEADS // N_KV_HEADS * WINDOW, WINDOW), 1)).astype(BF16)
    group = N_Q_HEADS // N_KV_HEADS

    units = [(i, g) for i in range(nblk) for g in range(N_KV_HEADS)]

    for u, (i, g) in enumerate(units):
        pieces = []
        for p in range(g * group // 2, (g + 1) * group // 2):
            qp = q_s[pl.ds(i * WINDOW, WINDOW), p * PAIR:(p + 1) * PAIR]
            zero = jnp.zeros_like(qp)
            pieces += [jnp.where(low_head, qp, zero), jnp.where(low_head, zero, qp)]
        lhs = jnp.concatenate([jnp.concatenate(pieces, axis=0), one_hot], axis=1)
        rhs = jnp.concatenate([k_s[g, pl.ds(i * WINDOW, 2 * WINDOW), :],
                               bias_first if i == 0 else bias], axis=1)
        s = lax.dot_general(lhs, rhs, (((1,), (1,)), ((), ())), preferred_element_type=F32)
        for j in range(group):
            sj = s[j * WINDOW:(j + 1) * WINDOW]
            s_s[u * group + j] = sj
            m = jnp.maximum(jnp.max(sj, axis=-1, keepdims=True), sinks_ref[g * group + j] * LOG2_E)
            m_s[u * group + j] = jnp.broadcast_to(m, (WINDOW, PAIR))

    def exponentials(u, i, g):
        for j in range(group):
            m = m_s[u * group + j]
            e = jnp.exp2(s_s[u * group + j] - jnp.concatenate([m, m], axis=1))
            p_s[u, pl.ds((j // 2) * WINDOW, WINDOW), pl.ds((j % 2) * 2 * WINDOW, 2 * WINDOW)] = e.astype(BF16)
        for jj in range(group // 2):
            sink_lo = sinks_ref[g * group + 2 * jj] * LOG2_E - m_s[u * group + 2 * jj]
            sink_hi = sinks_ref[g * group + 2 * jj + 1] * LOG2_E - m_s[u * group + 2 * jj + 1]
            t_s[u * (group // 2) + jj] = jnp.exp2(jnp.where(low_head, sink_lo, sink_hi))

    def weighted_values(u, i, g):
        krows = pl.ds(i * WINDOW, 2 * WINDOW)
        rhs = jnp.concatenate([v_s[2 * g, krows, :], v_s[2 * g + 1, krows, :]], axis=0)
        o = jnp.dot(p_s[u], rhs, preferred_element_type=F32)
        for jj in range(group // 2):
            oj = o[jj * WINDOW:(jj + 1) * WINDOW]
            den = oj[:, PAIR:] + t_s[u * (group // 2) + jj]
            p = g * group // 2 + jj
            ya_s[pl.ds(i * WINDOW, WINDOW), p * PAIR:(p + 1) * PAIR] = oj[:, 0:PAIR] / den

    for u in range(len(units) + PV_LAG):
        if u < len(units):
            exponentials(u, *units[u])
        if u >= PV_LAG:
            weighted_values(u - PV_LAG, *units[u - PV_LAG])

    y_s[:, 0:D_ATTN] = _rms_norm(ya_s[...], ga_ref[...]).astype(BF16)
    o_ref[0] = x_ref[0] + mod[2:3] * jnp.dot(y_s[...], wo_ref[...], preferred_element_type=F32)


def _token_mixer(x, positions, mod, norm1_g, w_in, b_in, conv_w, sinks, ga, gc, w_o):
    bsz, seq, _ = x.shape
    tile = SEQ_TILE
    assert seq % tile == 0 and tile % WINDOW == 0
    inv_freq = ROPE_THETA ** (-jnp.arange(0, HEAD_DIM // 2, dtype=F32) / (HEAD_DIM // 2))
    const = lambda shape: pl.BlockSpec(shape, lambda b, t: (0,) * len(shape),
                                       pipeline_mode=pl.Buffered(1))
    nhb = (tile // WINDOW) * N_Q_HEADS
    resident = (2 * 2 * tile * D_MODEL * 4 + (D_MODEL * D_IN + D_MODEL * D_MODEL) * 2
                + tile * D_IN * 4 + tile * (D_ATTN * 6 + 6 * PAIR * 2)
                + nhb * WINDOW * WINDOW * (8 + 4 + 2 + 2))
    return pl.pallas_call(
        _mixer_kernel,
        grid=(bsz, seq // tile),
        in_specs=[
            pl.BlockSpec(memory_space=pltpu.SMEM),
            pl.BlockSpec((1, tile, D_MODEL), lambda b, t: (b, t, 0)),
            pl.BlockSpec((1, 1, tile), lambda b, t: (b, 0, t)),
            pl.BlockSpec((1, N_MOD, D_MODEL), lambda b, t: (b, 0, 0)),
            const((HEAD_DIM // 2, 1)),
            const((1, D_MODEL)),
            const((D_MODEL, D_IN)),
            const((1, D_IN)),
            const((3, D_CONV)),
            const((1, D_ATTN)),
            const((1, D_CONV)),
            const((D_MODEL, D_MODEL)),
        ],
        out_specs=pl.BlockSpec((1, tile, D_MODEL), lambda b, t: (b, t, 0)),
        out_shape=jax.ShapeDtypeStruct(x.shape, F32),
        scratch_shapes=[
            pltpu.VMEM((tile, D_ATTN), BF16),
            pltpu.VMEM((N_KV_HEADS, tile + WINDOW, PAIR), BF16),
            pltpu.VMEM((2 * N_KV_HEADS, tile + WINDOW, 2 * PAIR), BF16),
            pltpu.VMEM((tile, D_ATTN), F32),
            pltpu.VMEM((V7X_SUBLANES, D_CONV), F32),
            pltpu.VMEM((nhb, WINDOW, 2 * WINDOW), F32),
            pltpu.VMEM((nhb, WINDOW, PAIR), F32),
            pltpu.VMEM((nhb // 4, 2 * WINDOW, 4 * WINDOW), BF16),
            pltpu.VMEM((nhb // 2, WINDOW, PAIR), F32),
            pltpu.VMEM((tile, D_MODEL), BF16),
        ],
        compiler_params=pltpu.CompilerParams(
            dimension_semantics=("arbitrary", "arbitrary"),
            vmem_limit_bytes=_vmem_limit(resident)),
        name="token_mixer",
    )(sinks, x, positions.reshape(bsz, 1, seq), mod, inv_freq.reshape(-1, 1),
      norm1_g.reshape(1, -1), w_in, b_in.reshape(1, -1), conv_w, ga.reshape(1, -1),
      gc.reshape(1, -1), w_o)


def _ffn_kernel(x_ref, mod_ref, n2g_ref, wup_ref, cw_ref, cb_ref, wdn_ref, fg_ref, o_ref,
                f_s, gc_s):
    t = pl.program_id(1)
    tile = x_ref.shape[1]

    @pl.when(t == 0)
    def _():
        gc_s[...] = jnp.zeros(gc_s.shape, F32)

    x = x_ref[0]
    mod = mod_ref[0]
    h = _rms_norm(x, n2g_ref[...] * (1.0 + mod[4:5])) + mod[3:4]
    hb = h.astype(BF16)

    for c in range(D_FF // FF_CHUNK):
        cols = pl.ds(c * FF_CHUNK, FF_CHUNK)
        vcols = pl.ds(D_FF + c * FF_CHUNK, FF_CHUNK)
        if c == 0:
            blocks = [hb[0:HEAD_ROWS], hb[HEAD_ROWS:]]
            gate = jnp.concatenate([jnp.dot(b, wup_ref[:, cols], preferred_element_type=F32)
                                    for b in blocks], axis=0)
            val = jnp.concatenate([jnp.dot(b, wup_ref[:, vcols], preferred_element_type=F32)
                                   for b in blocks], axis=0)
        else:
            gate = jnp.dot(hb, wup_ref[:, cols], preferred_element_type=F32)
            val = jnp.dot(hb, wup_ref[:, vcols], preferred_element_type=F32)
        conv = _causal_conv3(gate, gc_s[:, cols], cw_ref[:, cols]) + cb_ref[:, cols]
        gc_s[:, cols] = gate[tile - V7X_SUBLANES:tile]
        f_s[:, cols] = (jax.nn.silu(conv) * val).astype(BF16)

    rb = tile // TAIL_ROW_BLOCKS
    for r in range(TAIL_ROW_BLOCKS):
        rows = pl.ds(r * rb, rb)
        x2 = x_ref[0, rows, :] + mod[5:6] * jnp.dot(f_s[rows, :], wdn_ref[...], preferred_element_type=F32)
        o_ref[0, rows, :] = _rms_norm(x2, fg_ref[...])


def _channel_mixer(x, mod, norm2_g, w_up, conv_w, conv_b, w_down, final_g):
    bsz, seq, _ = x.shape
    tile = SEQ_TILE
    assert seq % tile == 0 and D_FF % FF_CHUNK == 0
    const = lambda shape: pl.BlockSpec(shape, lambda b, t: (0,) * len(shape),
                                       pipeline_mode=pl.Buffered(1))
    resident = (2 * 2 * tile * D_MODEL * 4 + 3 * D_MODEL * D_FF * 2 + tile * D_FF * 2
                + 4 * tile * FF_CHUNK * 4)
    return pl.pallas_call(
        _ffn_kernel,
        grid=(bsz, seq // tile),
        in_specs=[
            pl.BlockSpec((1, tile, D_MODEL), lambda b, t: (b, t, 0)),
            pl.BlockSpec((1, N_MOD, D_MODEL), lambda b, t: (b, 0, 0)),
            const((1, D_MODEL)),
            const((D_MODEL, 2 * D_FF)),
            const((3, D_FF)),
            const((1, D_FF)),
            const((D_FF, D_MODEL)),
            const((1, D_MODEL)),
        ],
        out_specs=pl.BlockSpec((1, tile, D_MODEL), lambda b, t: (b, t, 0)),
        out_shape=jax.ShapeDtypeStruct(x.shape, F32),
        scratch_shapes=[
            pltpu.VMEM((tile, D_FF), BF16),
            pltpu.VMEM((V7X_SUBLANES, D_FF), F32),
        ],
        compiler_params=pltpu.CompilerParams(
            dimension_semantics=("arbitrary", "arbitrary"),
            vmem_limit_bytes=_vmem_limit(resident)),
        name="channel_mixer",
    )(x, mod, norm2_g.reshape(1, -1), w_up, conv_w, conv_b.reshape(1, -1), w_down,
      final_g.reshape(1, -1))


def kernel(x, c, positions, ada_w, ada_b, norm1_g, w_in, b_in, conv_w, attn_sinks, out_norm_attn_g, out_norm_conv_g, w_o, norm2_g, w_up, ffn_conv_w, ffn_conv_b, w_down, final_norm_g):
    depth = ada_w.shape[0]
    bsz = x.shape[0]
    for l in range(depth):
        mod = _modulation(c, ada_w[l], ada_b[l]).reshape(bsz, N_MOD, D_MODEL)
        x = _token_mixer(x, positions, mod, norm1_g[l], w_in[l].astype(BF16), b_in[l], conv_w[l],
                         attn_sinks[l], out_norm_attn_g[l], out_norm_conv_g[l], w_o[l].astype(BF16))
        last = l == depth - 1
        assert last, "the final RMSNorm is fused into the last layer's channel mixer"
        x = _channel_mixer(x, mod, norm2_g[l], w_up[l].astype(BF16), ffn_conv_w[l], ffn_conv_b[l],
                           w_down[l].astype(BF16), final_norm_g)
    return x
```

```python
import math

import jax
import jax.numpy as jnp
from jax import lax
from jax.experimental import pallas as pl
from jax.experimental.pallas import tpu as pltpu

D_MODEL = 1024
HEAD_DIM = 64
N_Q_HEADS = 8
N_KV_HEADS = 2
D_ATTN = N_Q_HEADS * HEAD_DIM
D_KV = N_KV_HEADS * HEAD_DIM
D_CONV = D_MODEL - D_ATTN
D_QKV = D_ATTN + 2 * D_KV
D_IN = D_QKV + 3 * D_CONV
WINDOW = 128
D_FF = 2816
N_MOD = 6
ROPE_THETA = 10000.0
EPS = 1e-6
MASK_VALUE = -1e30
LOG2_E = math.log2(math.e)

V7X_SUBLANES = 8
V7X_MXU_COLS = 256
V7X_VMEM_BYTES = 64 * 1024 * 1024
V7X_VMEM_RESERVE = 4 * 1024 * 1024

SEQ_TILE = 1024
FF_CHUNK = V7X_MXU_COLS
PAIR = 2 * HEAD_DIM
HEAD_ROWS = 128
TAIL_ROW_BLOCKS = 4

F32 = jnp.float32
BF16 = jnp.bfloat16


def _vmem_limit(resident_bytes):
    return min(2 * resident_bytes, V7X_VMEM_BYTES - V7X_VMEM_RESERVE)


def _rms_norm(x, g):
    ms = jnp.mean(x * x, axis=-1, keepdims=True)
    return x * lax.rsqrt(ms + EPS) * g


def _modulated_norm(x, scale, shift):
    ms = jnp.mean(x * x, axis=-1, keepdims=True)
    return (x * lax.rsqrt(ms + EPS)).astype(BF16) * scale.astype(BF16) + shift.astype(BF16)


def _causal_conv3(cur, prev, w):
    rows = lax.broadcasted_iota(jnp.int32, (V7X_SUBLANES, cur.shape[1]), 0)
    p1 = prev[V7X_SUBLANES - 1:V7X_SUBLANES]
    p2 = prev[V7X_SUBLANES - 2:V7X_SUBLANES - 1]
    r1 = pltpu.roll(cur, 1, 0)
    r2 = pltpu.roll(cur, 2, 0)
    top1 = jnp.where(rows == 0, p1, r1[0:V7X_SUBLANES])
    top2 = jnp.where(rows == 0, p2, jnp.where(rows == 1, p1, r2[0:V7X_SUBLANES]))
    r1 = jnp.concatenate([top1, r1[V7X_SUBLANES:]], axis=0)
    r2 = jnp.concatenate([top2, r2[V7X_SUBLANES:]], axis=0)
    return r2 * w[0:1] + r1 * w[1:2] + cur * w[2:3]


def _mod_kernel(c_ref, w_ref, b_ref, o_ref):
    c_act = jax.nn.silu(c_ref[...])
    o_ref[...] = jnp.dot(c_act.astype(BF16), w_ref[...].astype(BF16),
                         preferred_element_type=F32) + b_ref[...]


def _modulation(c, ada_w, ada_b):
    bsz = c.shape[0]
    n = ada_w.shape[1]
    blk = D_MODEL
    return pl.pallas_call(
        _mod_kernel,
        grid=(n // blk,),
        in_specs=[
            pl.BlockSpec((bsz, D_MODEL), lambda j: (0, 0)),
            pl.BlockSpec((D_MODEL, blk), lambda j: (0, j)),
            pl.BlockSpec((1, blk), lambda j: (0, j)),
        ],
        out_specs=pl.BlockSpec((bsz, blk), lambda j: (0, j)),
        out_shape=jax.ShapeDtypeStruct((bsz, n), F32),
        name="adaln_modulation",
    )(c, ada_w, ada_b.reshape(1, n))


def _mixer_kernel(sinks_ref, x_ref, pos_ref, mod_ref, invf_ref, n1g_ref, win_ref, bin_ref,
                  cw_ref, ga_ref, gc_ref, wo_ref, o_ref,
                  q_s, k_s, v_s, ya_s, cc_s, s_s, m_s, p_s, t_s, y_s):
    t = pl.program_id(1)
    tile = x_ref.shape[1]
    nblk = tile // WINDOW

    lane = lax.broadcasted_iota(jnp.int32, (1, PAIR), 1)
    low_head = lane < HEAD_DIM
    first_half = (lane % HEAD_DIM) < (HEAD_DIM // 2)

    @pl.when(t > 0)
    def _():
        k_s[:, 0:WINDOW, :] = k_s[:, tile:tile + WINDOW, :]
        v_s[:, 0:WINDOW, 0:PAIR] = v_s[:, tile:tile + WINDOW, 0:PAIR]

    @pl.when(t == 0)
    def _():
        k_s[:, 0:WINDOW, :] = jnp.zeros((N_KV_HEADS, WINDOW, PAIR), BF16)
        v_s[:, 0:WINDOW, 0:PAIR] = jnp.zeros((2 * N_KV_HEADS, WINDOW, PAIR), BF16)
        ones_low = jnp.broadcast_to(jnp.where(low_head, 1.0, 0.0).astype(BF16), (tile + WINDOW, PAIR))
        ones_high = jnp.broadcast_to(jnp.where(low_head, 0.0, 1.0).astype(BF16), (tile + WINDOW, PAIR))
        for g in range(N_KV_HEADS):
            v_s[2 * g, :, PAIR:] = ones_low
            v_s[2 * g + 1, :, PAIR:] = ones_high
        cc_s[...] = jnp.zeros(cc_s.shape, F32)

    x = x_ref[0]
    mod = mod_ref[0]
    h = _rms_norm(x, n1g_ref[...] * (1.0 + mod[1:2])) + mod[0:1]
    hb = h.astype(BF16)

    qkv = jnp.dot(hb, win_ref[:, 0:D_QKV], preferred_element_type=F32) + bin_ref[:, 0:D_QKV]

    ang = invf_ref[...] * pos_ref[0].astype(F32)
    cos_f, sin_f = jnp.cos(ang), jnp.sin(ang)
    cos_t = jnp.concatenate([cos_f, cos_f, cos_f, cos_f], axis=0).T
    sin_t = jnp.concatenate([-sin_f, sin_f, -sin_f, sin_f], axis=0).T

    def rope(u):
        rot = jnp.where(first_half, pltpu.roll(u, PAIR - HEAD_DIM // 2, 1),
                        pltpu.roll(u, HEAD_DIM // 2, 1))
        return u * cos_t + rot * sin_t

    scale = LOG2_E / math.sqrt(HEAD_DIM)
    for p in range(D_ATTN // PAIR):
        q_s[:, p * PAIR:(p + 1) * PAIR] = (rope(qkv[:, p * PAIR:(p + 1) * PAIR]) * scale).astype(BF16)

    k = rope(qkv[:, D_ATTN:D_ATTN + D_KV])
    k_sw = pltpu.roll(k, HEAD_DIM, 1)
    v = qkv[:, D_ATTN + D_KV:D_QKV]
    v_sw = pltpu.roll(v, HEAD_DIM, 1)
    cur = pl.ds(WINDOW, tile)
    k_s[0, cur, :] = jnp.where(low_head, k, k_sw).astype(BF16)
    k_s[1, cur, :] = jnp.where(low_head, k_sw, k).astype(BF16)
    v_s[0, cur, 0:PAIR] = jnp.where(low_head, v, 0.0).astype(BF16)
    v_s[1, cur, 0:PAIR] = jnp.where(low_head, 0.0, v_sw).astype(BF16)
    v_s[2, cur, 0:PAIR] = jnp.where(low_head, v_sw, 0.0).astype(BF16)
    v_s[3, cur, 0:PAIR] = jnp.where(low_head, 0.0, v).astype(BF16)

    cproj = jnp.dot(hb, win_ref[:, D_QKV:D_IN], preferred_element_type=F32) + bin_ref[:, D_QKV:D_IN]
    bx = cproj[:, 0:D_CONV] * cproj[:, 2 * D_CONV:3 * D_CONV]
    y_conv = cproj[:, D_CONV:2 * D_CONV] * _causal_conv3(bx, cc_s[...], cw_ref[...])
    cc_s[...] = bx[tile - V7X_SUBLANES:tile]
    y_s[:, D_ATTN:] = _rms_norm(y_conv, gc_ref[...]).astype(BF16)

    kj = lax.broadcasted_iota(jnp.int32, (2 * WINDOW, WINDOW), 0)
    qi = lax.broadcasted_iota(jnp.int32, (2 * WINDOW, WINDOW), 1)
    diff = qi + WINDOW - kj
    band = (diff >= 0) & (diff < WINDOW)
    bias = jnp.where(band, 0.0, MASK_VALUE).astype(BF16)
    bias_first = jnp.where(band & ((kj >= WINDOW) | (t > 0)), 0.0, MASK_VALUE).astype(BF16)
    one_hot = (lax.broadcasted_iota(jnp.int32, (N_Q_HEADS // N_KV_HEADS * WINDOW, WINDOW), 0) % WINDOW
               == lax.broadcasted_iota(jnp.int32, (N_Q_HEADS // N_KV_HEADS * WINDOW, WINDOW), 1)).astype(BF16)
    group = N_Q_HEADS // N_KV_HEADS

    units = [(i, g) for i in range(nblk) for g in range(N_KV_HEADS)]

    for u, (i, g) in enumerate(units):
        pieces = []
        for p in range(g * group // 2, (g + 1) * group // 2):
            qp = q_s[pl.ds(i * WINDOW, WINDOW), p * PAIR:(p + 1) * PAIR]
            zero = jnp.zeros_like(qp)
            pieces += [jnp.where(low_head, qp, zero), jnp.where(low_head, zero, qp)]
        lhs = jnp.concatenate([jnp.concatenate(pieces, axis=0), one_hot], axis=1)
        rhs = jnp.concatenate([k_s[g, pl.ds(i * WINDOW, 2 * WINDOW), :],
                               bias_first if i == 0 else bias], axis=1)
        s = lax.dot_general(lhs, rhs, (((1,), (1,)), ((), ())), preferred_element_type=F32)
        for j in range(group):
            sj = s[j * WINDOW:(j + 1) * WINDOW]
            s_s[u * group + j] = sj
            m = jnp.maximum(jnp.max(sj, axis=-1, keepdims=True), sinks_ref[g * group + j] * LOG2_E)
            m_s[u * group + j] = jnp.broadcast_to(m, (WINDOW, PAIR))

    def exponentials(u, i, g):
        for j in range(group):
            m = m_s[u * group + j]
            e = jnp.exp2(s_s[u * group + j] - jnp.concatenate([m, m], axis=1))
            p_s[u, pl.ds((j // 2) * WINDOW, WINDOW), pl.ds((j % 2) * 2 * WINDOW, 2 * WINDOW)] = e.astype(BF16)
        for jj in range(group // 2):
            sink_lo = sinks_ref[g * group + 2 * jj] * LOG2_E - m_s[u * group + 2 * jj]
            sink_hi = sinks_ref[g * group + 2 * jj + 1] * LOG2_E - m_s[u * group + 2 * jj + 1]
            t_s[u * (group // 2) + jj] = jnp.exp2(jnp.where(low_head, sink_lo, sink_hi))

    def weighted_values(u, i, g):
        krows = pl.ds(i * WINDOW, 2 * WINDOW)
        rhs = jnp.concatenate([v_s[2 * g, krows, :], v_s[2 * g + 1, krows, :]], axis=0)
        o = jnp.dot(p_s[u], rhs, preferred_element_type=F32)
        for jj in range(group // 2):
            oj = o[jj * WINDOW:(jj + 1) * WINDOW]
            den = oj[:, PAIR:] + t_s[u * (group // 2) + jj]
            p = g * group // 2 + jj
            ya_s[pl.ds(i * WINDOW, WINDOW), p * PAIR:(p + 1) * PAIR] = oj[:, 0:PAIR] / den

    for u, (i, g) in enumerate(units):
        exponentials(u, i, g)
    for u, (i, g) in enumerate(units):
        weighted_values(u, i, g)

    y_s[:, 0:D_ATTN] = _rms_norm(ya_s[...], ga_ref[...]).astype(BF16)
    o_ref[0] = x_ref[0] + mod[2:3] * jnp.dot(y_s[...], wo_ref[...], preferred_element_type=F32)


def _token_mixer(x, positions, mod, norm1_g, w_in, b_in, conv_w, sinks, ga, gc, w_o):
    bsz, seq, _ = x.shape
    tile = SEQ_TILE
    assert seq % tile == 0 and tile % WINDOW == 0
    inv_freq = ROPE_THETA ** (-jnp.arange(0, HEAD_DIM // 2, dtype=F32) / (HEAD_DIM // 2))
    const = lambda shape: pl.BlockSpec(shape, lambda b, t: (0,) * len(shape),
                                       pipeline_mode=pl.Buffered(1))
    nhb = (tile // WINDOW) * N_Q_HEADS
    resident = (2 * 2 * tile * D_MODEL * 4 + (D_MODEL * D_IN + D_MODEL * D_MODEL) * 2
                + tile * D_IN * 4 + tile * (D_ATTN * 6 + 6 * PAIR * 2)
                + nhb * WINDOW * WINDOW * (8 + 4 + 2 + 2))
    return pl.pallas_call(
        _mixer_kernel,
        grid=(bsz, seq // tile),
        in_specs=[
            pl.BlockSpec(memory_space=pltpu.SMEM),
            pl.BlockSpec((1, tile, D_MODEL), lambda b, t: (b, t, 0)),
            pl.BlockSpec((1, 1, tile), lambda b, t: (b, 0, t)),
            pl.BlockSpec((1, N_MOD, D_MODEL), lambda b, t: (b, 0, 0)),
            const((HEAD_DIM // 2, 1)),
            const((1, D_MODEL)),
            const((D_MODEL, D_IN)),
            const((1, D_IN)),
            const((3, D_CONV)),
            const((1, D_ATTN)),
            const((1, D_CONV)),
            const((D_MODEL, D_MODEL)),
        ],
        out_specs=pl.BlockSpec((1, tile, D_MODEL), lambda b, t: (b, t, 0)),
        out_shape=jax.ShapeDtypeStruct(x.shape, F32),
        scratch_shapes=[
            pltpu.VMEM((tile, D_ATTN), BF16),
            pltpu.VMEM((N_KV_HEADS, tile + WINDOW, PAIR), BF16),
            pltpu.VMEM((2 * N_KV_HEADS, tile + WINDOW, 2 * PAIR), BF16),
            pltpu.VMEM((tile, D_ATTN), F32),
            pltpu.VMEM((V7X_SUBLANES, D_CONV), F32),
            pltpu.VMEM((nhb, WINDOW, 2 * WINDOW), F32),
            pltpu.VMEM((nhb, WINDOW, PAIR), F32),
            pltpu.VMEM((nhb // 4, 2 * WINDOW, 4 * WINDOW), BF16),
            pltpu.VMEM((nhb // 2, WINDOW, PAIR), F32),
            pltpu.VMEM((tile, D_MODEL), BF16),
        ],
        compiler_params=pltpu.CompilerParams(
            dimension_semantics=("arbitrary", "arbitrary"),
            vmem_limit_bytes=_vmem_limit(resident)),
        name="token_mixer",
    )(sinks, x, positions.reshape(bsz, 1, seq), mod, inv_freq.reshape(-1, 1),
      norm1_g.reshape(1, -1), w_in, b_in.reshape(1, -1), conv_w, ga.reshape(1, -1),
      gc.reshape(1, -1), w_o)


def _ffn_kernel(x_ref, mod_ref, n2g_ref, wup_ref, cw_ref, cb_ref, wdn_ref, fg_ref, o_ref,
                f_s, gc_s):
    t = pl.program_id(1)
    tile = x_ref.shape[1]

    @pl.when(t == 0)
    def _():
        gc_s[...] = jnp.zeros(gc_s.shape, F32)

    x = x_ref[0]
    mod = mod_ref[0]
    hb = _modulated_norm(x, n2g_ref[...] * (1.0 + mod[4:5]), mod[3:4])

    for c in range(D_FF // FF_CHUNK):
        cols = pl.ds(c * FF_CHUNK, FF_CHUNK)
        vcols = pl.ds(D_FF + c * FF_CHUNK, FF_CHUNK)
        if c == 0:
            blocks = [hb[0:HEAD_ROWS], hb[HEAD_ROWS:]]
            gate = jnp.concatenate([jnp.dot(b, wup_ref[:, cols], preferred_element_type=F32)
                                    for b in blocks], axis=0)
            val = jnp.concatenate([jnp.dot(b, wup_ref[:, vcols], preferred_element_type=F32)
                                   for b in blocks], axis=0)
        else:
            gate = jnp.dot(hb, wup_ref[:, cols], preferred_element_type=F32)
            val = jnp.dot(hb, wup_ref[:, vcols], preferred_element_type=F32)
        conv = _causal_conv3(gate, gc_s[:, cols], cw_ref[:, cols]) + cb_ref[:, cols]
        gc_s[:, cols] = gate[tile - V7X_SUBLANES:tile]
        f_s[:, cols] = (jax.nn.silu(conv) * val).astype(BF16)

    rb = tile // TAIL_ROW_BLOCKS
    for r in range(TAIL_ROW_BLOCKS):
        rows = pl.ds(r * rb, rb)
        x2 = x_ref[0, rows, :] + mod[5:6] * jnp.dot(f_s[rows, :], wdn_ref[...], preferred_element_type=F32)
        o_ref[0, rows, :] = _rms_norm(x2, fg_ref[...])


def _channel_mixer(x, mod, norm2_g, w_up, conv_w, conv_b, w_down, final_g):
    bsz, seq, _ = x.shape
    tile = SEQ_TILE
    assert seq % tile == 0 and D_FF % FF_CHUNK == 0
    const = lambda shape: pl.BlockSpec(shape, lambda b, t: (0,) * len(shape),
                                       pipeline_mode=pl.Buffered(1))
    resident = (2 * 2 * tile * D_MODEL * 4 + 3 * D_MODEL * D_FF * 2 + tile * D_FF * 2
                + 4 * tile * FF_CHUNK * 4)
    return pl.pallas_call(
        _ffn_kernel,
        grid=(bsz, seq // tile),
        in_specs=[
            pl.BlockSpec((1, tile, D_MODEL), lambda b, t: (b, t, 0)),
            pl.BlockSpec((1, N_MOD, D_MODEL), lambda b, t: (b, 0, 0)),
            const((1, D_MODEL)),
            const((D_MODEL, 2 * D_FF)),
            const((3, D_FF)),
            const((1, D_FF)),
            const((D_FF, D_MODEL)),
            const((1, D_MODEL)),
        ],
        out_specs=pl.BlockSpec((1, tile, D_MODEL), lambda b, t: (b, t, 0)),
        out_shape=jax.ShapeDtypeStruct(x.shape, F32),
        scratch_shapes=[
            pltpu.VMEM((tile, D_FF), BF16),
            pltpu.VMEM((V7X_SUBLANES, D_FF), F32),
        ],
        compiler_params=pltpu.CompilerParams(
            dimension_semantics=("arbitrary", "arbitrary"),
            vmem_limit_bytes=_vmem_limit(resident)),
        name="channel_mixer",
    )(x, mod, norm2_g.reshape(1, -1), w_up, conv_w, conv_b.reshape(1, -1), w_down,
      final_g.reshape(1, -1))


def kernel(x, c, positions, ada_w, ada_b, norm1_g, w_in, b_in, conv_w, attn_sinks, out_norm_attn_g, out_norm_conv_g, w_o, norm2_g, w_up, ffn_conv_w, ffn_conv_b, w_down, final_norm_g):
    depth = ada_w.shape[0]
    bsz = x.shape[0]
    for l in range(depth):
        mod = _modulation(c, ada_w[l], ada_b[l]).reshape(bsz, N_MOD, D_MODEL)
        x = _token_mixer(x, positions, mod, norm1_g[l], w_in[l].astype(BF16), b_in[l], conv_w[l],
                         attn_sinks[l], out_norm_attn_g[l], out_norm_conv_g[l], w_o[l].astype(BF16))
        last = l == depth - 1
        assert last, "the final RMSNorm is fused into the last layer's channel mixer"
        x = _channel_mixer(x, mod, norm2_g[l], w_up[l].astype(BF16), ffn_conv_w[l], ffn_conv_b[l],
                           w_down[l].astype(BF16), final_norm_g)
    return x
```

```python
import math

import jax
import jax.numpy as jnp
from jax import lax
from jax.experimental import pallas as pl
from jax.experimental.pallas import tpu as pltpu

D_MODEL = 1024
HEAD_DIM = 64
N_Q_HEADS = 8
N_KV_HEADS = 2
D_ATTN = N_Q_HEADS * HEAD_DIM
D_KV = N_KV_HEADS * HEAD_DIM
D_CONV = D_MODEL - D_ATTN
D_QKV = D_ATTN + 2 * D_KV
D_IN = D_QKV + 3 * D_CONV
WINDOW = 128
D_FF = 2816
N_MOD = 6
ROPE_THETA = 10000.0
EPS = 1e-6
MASK_VALUE = -1e30
LOG2_E = math.log2(math.e)

V7X_SUBLANES = 8
V7X_MXU_COLS = 256
V7X_VMEM_BYTES = 64 * 1024 * 1024
V7X_VMEM_RESERVE = 4 * 1024 * 1024

SEQ_TILE = 1024
FF_CHUNK = V7X_MXU_COLS
PAIR = 2 * HEAD_DIM
HEAD_ROWS = 128
TAIL_ROW_BLOCKS = 4

F32 = jnp.float32
BF16 = jnp.bfloat16


def _vmem_limit(resident_bytes):
    return min(2 * resident_bytes, V7X_VMEM_BYTES - V7X_VMEM_RESERVE)


def _rms_norm(x, g):
    ms = jnp.mean(x * x, axis=-1, keepdims=True)
    return x * lax.rsqrt(ms + EPS) * g


def _modulated_norm(x, scale, shift):
    ms = jnp.mean(x * x, axis=-1, keepdims=True)
    return (x * lax.rsqrt(ms + EPS)).astype(BF16) * scale.astype(BF16) + shift.astype(BF16)


def _causal_conv3(cur, prev, w):
    rows = lax.broadcasted_iota(jnp.int32, (V7X_SUBLANES, cur.shape[1]), 0)
    p1 = prev[V7X_SUBLANES - 1:V7X_SUBLANES]
    p2 = prev[V7X_SUBLANES - 2:V7X_SUBLANES - 1]
    r1 = pltpu.roll(cur, 1, 0)
    r2 = pltpu.roll(cur, 2, 0)
    top1 = jnp.where(rows == 0, p1, r1[0:V7X_SUBLANES])
    top2 = jnp.where(rows == 0, p2, jnp.where(rows == 1, p1, r2[0:V7X_SUBLANES]))
    r1 = jnp.concatenate([top1, r1[V7X_SUBLANES:]], axis=0)
    r2 = jnp.concatenate([top2, r2[V7X_SUBLANES:]], axis=0)
    return r2 * w[0:1] + r1 * w[1:2] + cur * w[2:3]


def _mod_kernel(c_ref, w_ref, b_ref, o_ref):
    c_act = jax.nn.silu(c_ref[...])
    o_ref[...] = jnp.dot(c_act.astype(BF16), w_ref[...].astype(BF16),
                         preferred_element_type=F32) + b_ref[...]


def _modulation(c, ada_w, ada_b):
    bsz = c.shape[0]
    n = ada_w.shape[1]
    blk = D_MODEL
    return pl.pallas_call(
        _mod_kernel,
        grid=(n // blk,),
        in_specs=[
            pl.BlockSpec((bsz, D_MODEL), lambda j: (0, 0)),
            pl.BlockSpec((D_MODEL, blk), lambda j: (0, j)),
            pl.BlockSpec((1, blk), lambda j: (0, j)),
        ],
        out_specs=pl.BlockSpec((bsz, blk), lambda j: (0, j)),
        out_shape=jax.ShapeDtypeStruct((bsz, n), F32),
        name="adaln_modulation",
    )(c, ada_w, ada_b.reshape(1, n))


def _mixer_kernel(sinks_ref, x_ref, pos_ref, mod_ref, invf_ref, n1g_ref, win_ref, bin_ref,
                  cw_ref, ga_ref, gc_ref, wo_ref, o_ref,
                  q_s, k_s, v_s, ya_s, cc_s, s_s, m_s, p_s, t_s, y_s):
    t = pl.program_id(1)
    tile = x_ref.shape[1]
    nblk = tile // WINDOW

    lane = lax.broadcasted_iota(jnp.int32, (1, PAIR), 1)
    low_head = lane < HEAD_DIM
    first_half = (lane % HEAD_DIM) < (HEAD_DIM // 2)

    @pl.when(t > 0)
    def _():
        k_s[:, 0:WINDOW, :] = k_s[:, tile:tile + WINDOW, :]
        v_s[:, 0:WINDOW, 0:PAIR] = v_s[:, tile:tile + WINDOW, 0:PAIR]

    @pl.when(t == 0)
    def _():
        k_s[:, 0:WINDOW, :] = jnp.zeros((N_KV_HEADS, WINDOW, PAIR), BF16)
        v_s[:, 0:WINDOW, 0:PAIR] = jnp.zeros((2 * N_KV_HEADS, WINDOW, PAIR), BF16)
        ones_low = jnp.broadcast_to(jnp.where(low_head, 1.0, 0.0).astype(BF16), (tile + WINDOW, PAIR))
        ones_high = jnp.broadcast_to(jnp.where(low_head, 0.0, 1.0).astype(BF16), (tile + WINDOW, PAIR))
        for g in range(N_KV_HEADS):
            v_s[2 * g, :, PAIR:] = ones_low
            v_s[2 * g + 1, :, PAIR:] = ones_high
        cc_s[...] = jnp.zeros(cc_s.shape, F32)

    x = x_ref[0]
    mod = mod_ref[0]
    h = _rms_norm(x, n1g_ref[...] * (1.0 + mod[1:2])) + mod[0:1]
    hb = h.astype(BF16)

    qkv = jnp.dot(hb, win_ref[:, 0:D_QKV], preferred_element_type=F32) + bin_ref[:, 0:D_QKV]

    ang = invf_ref[...] * pos_ref[0].astype(F32)
    cos_f, sin_f = jnp.cos(ang), jnp.sin(ang)
    cos_t = jnp.concatenate([cos_f, cos_f, cos_f, cos_f], axis=0).T
    sin_t = jnp.concatenate([-sin_f, sin_f, -sin_f, sin_f], axis=0).T

    def rope(u):
        rot = jnp.where(first_half, pltpu.roll(u, PAIR - HEAD_DIM // 2, 1),
                        pltpu.roll(u, HEAD_DIM // 2, 1))
        return u * cos_t + rot * sin_t

    scale = LOG2_E / math.sqrt(HEAD_DIM)
    for p in range(D_ATTN // PAIR):
        q_s[:, p * PAIR:(p + 1) * PAIR] = (rope(qkv[:, p * PAIR:(p + 1) * PAIR]) * scale).astype(BF16)

    k = rope(qkv[:, D_ATTN:D_ATTN + D_KV])
    k_sw = pltpu.roll(k, HEAD_DIM, 1)
    v = qkv[:, D_ATTN + D_KV:D_QKV]
    v_sw = pltpu.roll(v, HEAD_DIM, 1)
    cur = pl.ds(WINDOW, tile)
    k_s[0, cur, :] = jnp.where(low_head, k, k_sw).astype(BF16)
    k_s[1, cur, :] = jnp.where(low_head, k_sw, k).astype(BF16)
    v_s[0, cur, 0:PAIR] = jnp.where(low_head, v, 0.0).astype(BF16)
    v_s[1, cur, 0:PAIR] = jnp.where(low_head, 0.0, v_sw).astype(BF16)
    v_s[2, cur, 0:PAIR] = jnp.where(low_head, v_sw, 0.0).astype(BF16)
    v_s[3, cur, 0:PAIR] = jnp.where(low_head, 0.0, v).astype(BF16)

    cproj = jnp.dot(hb, win_ref[:, D_QKV:D_IN], preferred_element_type=F32) + bin_ref[:, D_QKV:D_IN]
    bx = cproj[:, 0:D_CONV] * cproj[:, 2 * D_CONV:3 * D_CONV]
    y_conv = cproj[:, D_CONV:2 * D_CONV] * _causal_conv3(bx, cc_s[...], cw_ref[...])
    cc_s[...] = bx[tile - V7X_SUBLANES:tile]
    y_s[:, D_ATTN:] = _rms_norm(y_conv, gc_ref[...]).astype(BF16)

    kj = lax.broadcasted_iota(jnp.int32, (2 * WINDOW, WINDOW), 0)
    qi = lax.broadcasted_iota(jnp.int32, (2 * WINDOW, WINDOW), 1)
    diff = qi + WINDOW - kj
    band = (diff >= 0) & (diff < WINDOW)
    bias = jnp.where(band, 0.0, MASK_VALUE).astype(BF16)
    bias_first = jnp.where(band & ((kj >= WINDOW) | (t > 0)), 0.0, MASK_VALUE).astype(BF16)
    one_hot = (lax.broadcasted_iota(jnp.int32, (N_Q_HEADS // N_KV_HEADS * WINDOW, WINDOW), 0) % WINDOW
               == lax.broadcasted_iota(jnp.int32, (N_Q_HEADS // N_KV_HEADS * WINDOW, WINDOW), 1)).astype(BF16)
    group = N_Q_HEADS // N_KV_HEADS

    units = [(i, p) for i in range(nblk) for p in range(D_ATTN // PAIR)]

    for u, (i, p) in enumerate(units):
        g = p // (group // 2)
        qp = q_s[pl.ds(i * WINDOW, WINDOW), p * PAIR:(p + 1) * PAIR]
        zero = jnp.zeros_like(qp)
        lhs = jnp.concatenate([jnp.concatenate([jnp.where(low_head, qp, zero), jnp.where(low_head, zero, qp)],
                                               axis=0), one_hot[0:2 * WINDOW]], axis=1)
        rhs = jnp.concatenate([k_s[g, pl.ds(i * WINDOW, 2 * WINDOW), :],
                               bias_first if i == 0 else bias], axis=1)
        s = lax.dot_general(lhs, rhs, (((1,), (1,)), ((), ())), preferred_element_type=F32)
        for j in range(2):
            sj = s[j * WINDOW:(j + 1) * WINDOW]
            s_s[2 * u + j] = sj
            m = jnp.maximum(jnp.max(sj, axis=-1, keepdims=True), sinks_ref[2 * p + j] * LOG2_E)
            m_s[2 * u + j] = jnp.broadcast_to(m, (WINDOW, PAIR))

    def exponentials(u, i, p):
        for j in range(2):
            m = m_s[2 * u + j]
            e = jnp.exp2(s_s[2 * u + j] - jnp.concatenate([m, m], axis=1))
            p_s[u, :, pl.ds(j * 2 * WINDOW, 2 * WINDOW)] = e.astype(BF16)
        sink_lo = sinks_ref[2 * p] * LOG2_E - m_s[2 * u]
        sink_hi = sinks_ref[2 * p + 1] * LOG2_E - m_s[2 * u + 1]
        t_s[u] = jnp.exp2(jnp.where(low_head, sink_lo, sink_hi))

    def weighted_values(u, i, p):
        g = p // (group // 2)
        krows = pl.ds(i * WINDOW, 2 * WINDOW)
        rhs = jnp.concatenate([v_s[2 * g, krows, :], v_s[2 * g + 1, krows, :]], axis=0)
        o = jnp.dot(p_s[u], rhs, preferred_element_type=F32)
        ya_s[pl.ds(i * WINDOW, WINDOW), p * PAIR:(p + 1) * PAIR] = o[:, 0:PAIR] / (o[:, PAIR:] + t_s[u])

    for u, (i, p) in enumerate(units):
        exponentials(u, i, p)
    for u, (i, p) in enumerate(units):
        weighted_values(u, i, p)

    y_s[:, 0:D_ATTN] = _rms_norm(ya_s[...], ga_ref[...]).astype(BF16)
    o_ref[0] = x_ref[0] + mod[2:3] * jnp.dot(y_s[...], wo_ref[...], preferred_element_type=F32)


def _token_mixer(x, positions, mod, norm1_g, w_in, b_in, conv_w, sinks, ga, gc, w_o):
    bsz, seq, _ = x.shape
    tile = SEQ_TILE
    assert seq % tile == 0 and tile % WINDOW == 0
    inv_freq = ROPE_THETA ** (-jnp.arange(0, HEAD_DIM // 2, dtype=F32) / (HEAD_DIM // 2))
    const = lambda shape: pl.BlockSpec(shape, lambda b, t: (0,) * len(shape),
                                       pipeline_mode=pl.Buffered(1))
    nhb = (tile // WINDOW) * N_Q_HEADS
    resident = (2 * 2 * tile * D_MODEL * 4 + (D_MODEL * D_IN + D_MODEL * D_MODEL) * 2
                + tile * D_IN * 4 + tile * (D_ATTN * 6 + 6 * PAIR * 2)
                + nhb * WINDOW * WINDOW * (8 + 4 + 2 + 2))
    return pl.pallas_call(
        _mixer_kernel,
        grid=(bsz, seq // tile),
        in_specs=[
            pl.BlockSpec(memory_space=pltpu.SMEM),
            pl.BlockSpec((1, tile, D_MODEL), lambda b, t: (b, t, 0)),
            pl.BlockSpec((1, 1, tile), lambda b, t: (b, 0, t)),
            pl.BlockSpec((1, N_MOD, D_MODEL), lambda b, t: (b, 0, 0)),
            const((HEAD_DIM // 2, 1)),
            const((1, D_MODEL)),
            const((D_MODEL, D_IN)),
            const((1, D_IN)),
            const((3, D_CONV)),
            const((1, D_ATTN)),
            const((1, D_CONV)),
            const((D_MODEL, D_MODEL)),
        ],
        out_specs=pl.BlockSpec((1, tile, D_MODEL), lambda b, t: (b, t, 0)),
        out_shape=jax.ShapeDtypeStruct(x.shape, F32),
        scratch_shapes=[
            pltpu.VMEM((tile, D_ATTN), BF16),
            pltpu.VMEM((N_KV_HEADS, tile + WINDOW, PAIR), BF16),
            pltpu.VMEM((2 * N_KV_HEADS, tile + WINDOW, 2 * PAIR), BF16),
            pltpu.VMEM((tile, D_ATTN), F32),
            pltpu.VMEM((V7X_SUBLANES, D_CONV), F32),
            pltpu.VMEM((nhb, WINDOW, 2 * WINDOW), F32),
            pltpu.VMEM((nhb, WINDOW, PAIR), F32),
            pltpu.VMEM((nhb // 2, WINDOW, 4 * WINDOW), BF16),
            pltpu.VMEM((nhb // 2, WINDOW, PAIR), F32),
            pltpu.VMEM((tile, D_MODEL), BF16),
        ],
        compiler_params=pltpu.CompilerParams(
            dimension_semantics=("arbitrary", "arbitrary"),
            vmem_limit_bytes=_vmem_limit(resident)),
        name="token_mixer",
    )(sinks, x, positions.reshape(bsz, 1, seq), mod, inv_freq.reshape(-1, 1),
      norm1_g.reshape(1, -1), w_in, b_in.reshape(1, -1), conv_w, ga.reshape(1, -1),
      gc.reshape(1, -1), w_o)


def _ffn_kernel(x_ref, mod_ref, n2g_ref, wup_ref, cw_ref, cb_ref, wdn_ref, fg_ref, o_ref,
                f_s, gc_s):
    t = pl.program_id(1)
    tile = x_ref.shape[1]

    @pl.when(t == 0)
    def _():
        gc_s[...] = jnp.zeros(gc_s.shape, F32)

    x = x_ref[0]
    mod = mod_ref[0]
    hb = _modulated_norm(x, n2g_ref[...] * (1.0 + mod[4:5]), mod[3:4])

    for c in range(D_FF // FF_CHUNK):
        cols = pl.ds(c * FF_CHUNK, FF_CHUNK)
        vcols = pl.ds(D_FF + c * FF_CHUNK, FF_CHUNK)
        if c == 0:
            blocks = [hb[0:HEAD_ROWS], hb[HEAD_ROWS:]]
            gate = jnp.concatenate([jnp.dot(b, wup_ref[:, cols], preferred_element_type=F32)
                                    for b in blocks], axis=0)
            val = jnp.concatenate([jnp.dot(b, wup_ref[:, vcols], preferred_element_type=F32)
                                   for b in blocks], axis=0)
        else:
            gate = jnp.dot(hb, wup_ref[:, cols], preferred_element_type=F32)
            val = jnp.dot(hb, wup_ref[:, vcols], preferred_element_type=F32)
        conv = _causal_conv3(gate, gc_s[:, cols], cw_ref[:, cols]) + cb_ref[:, cols]
        gc_s[:, cols] = gate[tile - V7X_SUBLANES:tile]
        f_s[:, cols] = (jax.nn.silu(conv) * val).astype(BF16)

    rb = tile // TAIL_ROW_BLOCKS
    for r in range(TAIL_ROW_BLOCKS):
        rows = pl.ds(r * rb, rb)
        x2 = x_ref[0, rows, :] + mod[5:6] * jnp.dot(f_s[rows, :], wdn_ref[...], preferred_element_type=F32)
        o_ref[0, rows, :] = _rms_norm(x2, fg_ref[...])


def _channel_mixer(x, mod, norm2_g, w_up, conv_w, conv_b, w_down, final_g):
    bsz, seq, _ = x.shape
    tile = SEQ_TILE
    assert seq % tile == 0 and D_FF % FF_CHUNK == 0
    const = lambda shape: pl.BlockSpec(shape, lambda b, t: (0,) * len(shape),
                                       pipeline_mode=pl.Buffered(1))
    resident = (2 * 2 * tile * D_MODEL * 4 + 3 * D_MODEL * D_FF * 2 + tile * D_FF * 2
                + 4 * tile * FF_CHUNK * 4)
    return pl.pallas_call(
        _ffn_kernel,
        grid=(bsz, seq // tile),
        in_specs=[
            pl.BlockSpec((1, tile, D_MODEL), lambda b, t: (b, t, 0)),
            pl.BlockSpec((1, N_MOD, D_MODEL), lambda b, t: (b, 0, 0)),
            const((1, D_MODEL)),
            const((D_MODEL, 2 * D_FF)),
            const((3, D_FF)),
            const((1, D_FF)),
            const((D_FF, D_MODEL)),
            const((1, D_MODEL)),
        ],
        out_specs=pl.BlockSpec((1, tile, D_MODEL), lambda b, t: (b, t, 0)),
        out_shape=jax.ShapeDtypeStruct(x.shape, F32),
        scratch_shapes=[
            pltpu.VMEM((tile, D_FF), BF16),
            pltpu.VMEM((V7X_SUBLANES, D_FF), F32),
        ],
        compiler_params=pltpu.CompilerParams(
            dimension_semantics=("arbitrary", "arbitrary"),
            vmem_limit_bytes=_vmem_limit(resident)),
        name="channel_mixer",
    )(x, mod, norm2_g.reshape(1, -1), w_up, conv_w, conv_b.reshape(1, -1), w_down,
      final_g.reshape(1, -1))


def kernel(x, c, positions, ada_w, ada_b, norm1_g, w_in, b_in, conv_w, attn_sinks, out_norm_attn_g, out_norm_conv_g, w_o, norm2_g, w_up, ffn_conv_w, ffn_conv_b, w_down, final_norm_g):
    depth = ada_w.shape[0]
    bsz = x.shape[0]
    for l in range(depth):
        mod = _modulation(c, ada_w[l], ada_b[l]).reshape(bsz, N_MOD, D_MODEL)
        x = _token_mixer(x, positions, mod, norm1_g[l], w_in[l].astype(BF16), b_in[l], conv_w[l],
                         attn_sinks[l], out_norm_attn_g[l], out_norm_conv_g[l], w_o[l].astype(BF16))
        last = l == depth - 1
        assert last, "the final RMSNorm is fused into the last layer's channel mixer"
        x = _channel_mixer(x, mod, norm2_g[l], w_up[l].astype(BF16), ffn_conv_w[l], ffn_conv_b[l],
                           w_down[l].astype(BF16), final_norm_g)
    return x
```

```python
import math

import jax
import jax.numpy as jnp
from jax import lax
from jax.experimental import pallas as pl
from jax.experimental.pallas import tpu as pltpu

D_MODEL = 1024
HEAD_DIM = 64
N_Q_HEADS = 8
N_KV_HEADS = 2
D_ATTN = N_Q_HEADS * HEAD_DIM
D_KV = N_KV_HEADS * HEAD_DIM
D_CONV = D_MODEL - D_ATTN
D_QKV = D_ATTN + 2 * D_KV
D_IN = D_QKV + 3 * D_CONV
WINDOW = 128
D_FF = 2816
N_MOD = 6
ROPE_THETA = 10000.0
EPS = 1e-6
MASK_VALUE = -1e30
LOG2_E = math.log2(math.e)

V7X_SUBLANES = 8
V7X_MXU_COLS = 256
V7X_VMEM_BYTES = 64 * 1024 * 1024
V7X_VMEM_RESERVE = 4 * 1024 * 1024

SEQ_TILE = 1024
FF_CHUNK = V7X_MXU_COLS
PAIR = 2 * HEAD_DIM
HEAD_ROWS = 128
TAIL_ROW_BLOCKS = 4

F32 = jnp.float32
BF16 = jnp.bfloat16


def _vmem_limit(resident_bytes):
    return min(2 * resident_bytes, V7X_VMEM_BYTES - V7X_VMEM_RESERVE)


def _rms_norm(x, g):
    ms = jnp.mean(x * x, axis=-1, keepdims=True)
    return x * lax.rsqrt(ms + EPS) * g


def _modulated_norm(x, scale, shift):
    ms = jnp.mean(x * x, axis=-1, keepdims=True)
    return (x * lax.rsqrt(ms + EPS)).astype(BF16) * scale.astype(BF16) + shift.astype(BF16)


def _causal_conv3(cur, prev, w):
    rows = lax.broadcasted_iota(jnp.int32, (V7X_SUBLANES, cur.shape[1]), 0)
    p1 = prev[V7X_SUBLANES - 1:V7X_SUBLANES]
    p2 = prev[V7X_SUBLANES - 2:V7X_SUBLANES - 1]
    r1 = pltpu.roll(cur, 1, 0)
    r2 = pltpu.roll(cur, 2, 0)
    top1 = jnp.where(rows == 0, p1, r1[0:V7X_SUBLANES])
    top2 = jnp.where(rows == 0, p2, jnp.where(rows == 1, p1, r2[0:V7X_SUBLANES]))
    r1 = jnp.concatenate([top1, r1[V7X_SUBLANES:]], axis=0)
    r2 = jnp.concatenate([top2, r2[V7X_SUBLANES:]], axis=0)
    return r2 * w[0:1] + r1 * w[1:2] + cur * w[2:3]


def _mod_kernel(c_ref, w_ref, b_ref, o_ref):
    c_act = jax.nn.silu(c_ref[...])
    o_ref[...] = jnp.dot(c_act.astype(BF16), w_ref[...].astype(BF16),
                         preferred_element_type=F32) + b_ref[...]


def _modulation(c, ada_w, ada_b):
    bsz = c.shape[0]
    n = ada_w.shape[1]
    blk = D_MODEL
    return pl.pallas_call(
        _mod_kernel,
        grid=(n // blk,),
        in_specs=[
            pl.BlockSpec((bsz, D_MODEL), lambda j: (0, 0)),
            pl.BlockSpec((D_MODEL, blk), lambda j: (0, j)),
            pl.BlockSpec((1, blk), lambda j: (0, j)),
        ],
        out_specs=pl.BlockSpec((bsz, blk), lambda j: (0, j)),
        out_shape=jax.ShapeDtypeStruct((bsz, n), F32),
        name="adaln_modulation",
    )(c, ada_w, ada_b.reshape(1, n))


def _mixer_kernel(sinks_ref, x_ref, pos_ref, mod_ref, invf_ref, n1g_ref, win_ref, bin_ref,
                  cw_ref, ga_ref, gc_ref, wo_ref, o_ref,
                  q_s, k_s, v_s, ya_s, cc_s, s_s, m_s, p_s, t_s, y_s):
    t = pl.program_id(1)
    tile = x_ref.shape[1]
    nblk = tile // WINDOW

    lane = lax.broadcasted_iota(jnp.int32, (1, PAIR), 1)
    low_head = lane < HEAD_DIM
    first_half = (lane % HEAD_DIM) < (HEAD_DIM // 2)

    @pl.when(t > 0)
    def _():
        k_s[:, 0:WINDOW, :] = k_s[:, tile:tile + WINDOW, :]
        v_s[:, 0:WINDOW, 0:PAIR] = v_s[:, tile:tile + WINDOW, 0:PAIR]

    @pl.when(t == 0)
    def _():
        k_s[:, 0:WINDOW, :] = jnp.zeros((N_KV_HEADS, WINDOW, PAIR), BF16)
        v_s[:, 0:WINDOW, 0:PAIR] = jnp.zeros((N_KV_HEADS, WINDOW, PAIR), BF16)
        v_s[:, :, PAIR:] = jnp.ones((N_KV_HEADS, tile + WINDOW, PAIR), BF16)
        cc_s[...] = jnp.zeros(cc_s.shape, F32)

    x = x_ref[0]
    mod = mod_ref[0]
    h = _rms_norm(x, n1g_ref[...] * (1.0 + mod[1:2])) + mod[0:1]
    hb = h.astype(BF16)

    qkv = jnp.dot(hb, win_ref[:, 0:D_QKV], preferred_element_type=F32) + bin_ref[:, 0:D_QKV]

    ang = invf_ref[...] * pos_ref[0].astype(F32)
    cos_f, sin_f = jnp.cos(ang), jnp.sin(ang)
    cos_t = jnp.concatenate([cos_f, cos_f, cos_f, cos_f], axis=0).T
    sin_t = jnp.concatenate([-sin_f, sin_f, -sin_f, sin_f], axis=0).T

    def rope(u):
        rot = jnp.where(first_half, pltpu.roll(u, PAIR - HEAD_DIM // 2, 1),
                        pltpu.roll(u, HEAD_DIM // 2, 1))
        return u * cos_t + rot * sin_t

    scale = LOG2_E / math.sqrt(HEAD_DIM)
    for p in range(D_ATTN // PAIR):
        q_s[:, p * PAIR:(p + 1) * PAIR] = (rope(qkv[:, p * PAIR:(p + 1) * PAIR]) * scale).astype(BF16)

    k = rope(qkv[:, D_ATTN:D_ATTN + D_KV])
    k_sw = pltpu.roll(k, HEAD_DIM, 1)
    v = qkv[:, D_ATTN + D_KV:D_QKV]
    v_sw = pltpu.roll(v, HEAD_DIM, 1)
    cur = pl.ds(WINDOW, tile)
    k_s[0, cur, :] = jnp.where(low_head, k, k_sw).astype(BF16)
    k_s[1, cur, :] = jnp.where(low_head, k_sw, k).astype(BF16)
    v_s[0, cur, 0:PAIR] = jnp.where(low_head, v, v_sw).astype(BF16)
    v_s[1, cur, 0:PAIR] = jnp.where(low_head, v_sw, v).astype(BF16)

    cproj = jnp.dot(hb, win_ref[:, D_QKV:D_IN], preferred_element_type=F32) + bin_ref[:, D_QKV:D_IN]
    bx = cproj[:, 0:D_CONV] * cproj[:, 2 * D_CONV:3 * D_CONV]
    y_conv = cproj[:, D_CONV:2 * D_CONV] * _causal_conv3(bx, cc_s[...], cw_ref[...])
    cc_s[...] = bx[tile - V7X_SUBLANES:tile]
    y_s[:, D_ATTN:] = _rms_norm(y_conv, gc_ref[...]).astype(BF16)

    kj = lax.broadcasted_iota(jnp.int32, (2 * WINDOW, WINDOW), 0)
    qi = lax.broadcasted_iota(jnp.int32, (2 * WINDOW, WINDOW), 1)
    diff = qi + WINDOW - kj
    band = (diff >= 0) & (diff < WINDOW)
    bias = jnp.where(band, 0.0, MASK_VALUE).astype(BF16)
    bias_first = jnp.where(band & ((kj >= WINDOW) | (t > 0)), 0.0, MASK_VALUE).astype(BF16)
    one_hot = (lax.broadcasted_iota(jnp.int32, (N_Q_HEADS // N_KV_HEADS * WINDOW, WINDOW), 0) % WINDOW
               == lax.broadcasted_iota(jnp.int32, (N_Q_HEADS // N_KV_HEADS * WINDOW, WINDOW), 1)).astype(BF16)
    group = N_Q_HEADS // N_KV_HEADS

    units = [(i, g) for i in range(nblk) for g in range(N_KV_HEADS)]

    for u, (i, g) in enumerate(units):
        pieces = []
        for p in range(g * group // 2, (g + 1) * group // 2):
            qp = q_s[pl.ds(i * WINDOW, WINDOW), p * PAIR:(p + 1) * PAIR]
            zero = jnp.zeros_like(qp)
            pieces += [jnp.where(low_head, qp, zero), jnp.where(low_head, zero, qp)]
        lhs = jnp.concatenate([jnp.concatenate(pieces, axis=0), one_hot], axis=1)
        rhs = jnp.concatenate([k_s[g, pl.ds(i * WINDOW, 2 * WINDOW), :],
                               bias_first if i == 0 else bias], axis=1)
        s = lax.dot_general(lhs, rhs, (((1,), (1,)), ((), ())), preferred_element_type=F32)
        for j in range(group):
            sj = s[j * WINDOW:(j + 1) * WINDOW]
            s_s[u * group + j] = sj
            m = jnp.maximum(jnp.max(sj, axis=-1, keepdims=True), sinks_ref[g * group + j] * LOG2_E)
            m_s[u * group + j] = jnp.broadcast_to(m, (WINDOW, PAIR))

    def exponentials(u, i, g):
        for j in range(group):
            m = m_s[u * group + j]
            e = jnp.exp2(s_s[u * group + j] - jnp.concatenate([m, m], axis=1))
            p_s[u, pl.ds(j * WINDOW, WINDOW), :] = e.astype(BF16)
        for jj in range(group // 2):
            sink_lo = sinks_ref[g * group + 2 * jj] * LOG2_E - m_s[u * group + 2 * jj]
            sink_hi = sinks_ref[g * group + 2 * jj + 1] * LOG2_E - m_s[u * group + 2 * jj + 1]
            t_s[u * (group // 2) + jj] = jnp.exp2(jnp.where(low_head, sink_lo, sink_hi))

    def weighted_values(u, i, g):
        krows = pl.ds(i * WINDOW, 2 * WINDOW)
        o = jnp.dot(p_s[u], v_s[g, krows, :], preferred_element_type=F32)
        for jj in range(group // 2):
            oa = o[2 * jj * WINDOW:(2 * jj + 1) * WINDOW]
            ob = o[(2 * jj + 1) * WINDOW:(2 * jj + 2) * WINDOW]
            num = jnp.where(low_head, oa[:, 0:PAIR], ob[:, 0:PAIR])
            den = jnp.where(low_head, oa[:, PAIR:], ob[:, PAIR:]) + t_s[u * (group // 2) + jj]
            p = g * group // 2 + jj
            ya_s[pl.ds(i * WINDOW, WINDOW), p * PAIR:(p + 1) * PAIR] = num / den

    for u, (i, g) in enumerate(units):
        exponentials(u, i, g)
    for u, (i, g) in enumerate(units):
        weighted_values(u, i, g)

    y_s[:, 0:D_ATTN] = _rms_norm(ya_s[...], ga_ref[...]).astype(BF16)
    o_ref[0] = x_ref[0] + mod[2:3] * jnp.dot(y_s[...], wo_ref[...], preferred_element_type=F32)


def _token_mixer(x, positions, mod, norm1_g, w_in, b_in, conv_w, sinks, ga, gc, w_o):
    bsz, seq, _ = x.shape
    tile = SEQ_TILE
    assert seq % tile == 0 and tile % WINDOW == 0
    inv_freq = ROPE_THETA ** (-jnp.arange(0, HEAD_DIM // 2, dtype=F32) / (HEAD_DIM // 2))
    const = lambda shape: pl.BlockSpec(shape, lambda b, t: (0,) * len(shape),
                                       pipeline_mode=pl.Buffered(1))
    nhb = (tile // WINDOW) * N_Q_HEADS
    resident = (2 * 2 * tile * D_MODEL * 4 + (D_MODEL * D_IN + D_MODEL * D_MODEL) * 2
                + tile * D_IN * 4 + tile * (D_ATTN * 6 + 6 * PAIR * 2)
                + nhb * WINDOW * WINDOW * (8 + 4 + 2 + 2))
    return pl.pallas_call(
        _mixer_kernel,
        grid=(bsz, seq // tile),
        in_specs=[
            pl.BlockSpec(memory_space=pltpu.SMEM),
            pl.BlockSpec((1, tile, D_MODEL), lambda b, t: (b, t, 0)),
            pl.BlockSpec((1, 1, tile), lambda b, t: (b, 0, t)),
            pl.BlockSpec((1, N_MOD, D_MODEL), lambda b, t: (b, 0, 0)),
            const((HEAD_DIM // 2, 1)),
            const((1, D_MODEL)),
            const((D_MODEL, D_IN)),
            const((1, D_IN)),
            const((3, D_CONV)),
            const((1, D_ATTN)),
            const((1, D_CONV)),
            const((D_MODEL, D_MODEL)),
        ],
        out_specs=pl.BlockSpec((1, tile, D_MODEL), lambda b, t: (b, t, 0)),
        out_shape=jax.ShapeDtypeStruct(x.shape, F32),
        scratch_shapes=[
            pltpu.VMEM((tile, D_ATTN), BF16),
            pltpu.VMEM((N_KV_HEADS, tile + WINDOW, PAIR), BF16),
            pltpu.VMEM((N_KV_HEADS, tile + WINDOW, 2 * PAIR), BF16),
            pltpu.VMEM((tile, D_ATTN), F32),
            pltpu.VMEM((V7X_SUBLANES, D_CONV), F32),
            pltpu.VMEM((nhb, WINDOW, 2 * WINDOW), F32),
            pltpu.VMEM((nhb, WINDOW, PAIR), F32),
            pltpu.VMEM((nhb // 4, 4 * WINDOW, 2 * WINDOW), BF16),
            pltpu.VMEM((nhb // 2, WINDOW, PAIR), F32),
            pltpu.VMEM((tile, D_MODEL), BF16),
        ],
        compiler_params=pltpu.CompilerParams(
            dimension_semantics=("arbitrary", "arbitrary"),
            vmem_limit_bytes=_vmem_limit(resident)),
        name="token_mixer",
    )(sinks, x, positions.reshape(bsz, 1, seq), mod, inv_freq.reshape(-1, 1),
      norm1_g.reshape(1, -1), w_in, b_in.reshape(1, -1), conv_w, ga.reshape(1, -1),
      gc.reshape(1, -1), w_o)


def _ffn_kernel(x_ref, mod_ref, n2g_ref, wup_ref, cw_ref, cb_ref, wdn_ref, fg_ref, o_ref,
                f_s, gc_s):
    t = pl.program_id(1)
    tile = x_ref.shape[1]

    @pl.when(t == 0)
    def _():
        gc_s[...] = jnp.zeros(gc_s.shape, F32)

    x = x_ref[0]
    mod = mod_ref[0]
    hb = _modulated_norm(x, n2g_ref[...] * (1.0 + mod[4:5]), mod[3:4])

    for c in range(D_FF // FF_CHUNK):
        cols = pl.ds(c * FF_CHUNK, FF_CHUNK)
        vcols = pl.ds(D_FF + c * FF_CHUNK, FF_CHUNK)
        if c == 0:
            blocks = [hb[0:HEAD_ROWS], hb[HEAD_ROWS:]]
            gate = jnp.concatenate([jnp.dot(b, wup_ref[:, cols], preferred_element_type=F32)
                                    for b in blocks], axis=0)
            val = jnp.concatenate([jnp.dot(b, wup_ref[:, vcols], preferred_element_type=F32)
                                   for b in blocks], axis=0)
        else:
            gate = jnp.dot(hb, wup_ref[:, cols], preferred_element_type=F32)
            val = jnp.dot(hb, wup_ref[:, vcols], preferred_element_type=F32)
        conv = _causal_conv3(gate, gc_s[:, cols], cw_ref[:, cols]) + cb_ref[:, cols]
        gc_s[:, cols] = gate[tile - V7X_SUBLANES:tile]
        f_s[:, cols] = (jax.nn.silu(conv) * val).astype(BF16)

    rb = tile // TAIL_ROW_BLOCKS
    for r in range(TAIL_ROW_BLOCKS):
        rows = pl.ds(r * rb, rb)
        x2 = x_ref[0, rows, :] + mod[5:6] * jnp.dot(f_s[rows, :], wdn_ref[...], preferred_element_type=F32)
        o_ref[0, rows, :] = _rms_norm(x2, fg_ref[...])


def _channel_mixer(x, mod, norm2_g, w_up, conv_w, conv_b, w_down, final_g):
    bsz, seq, _ = x.shape
    tile = SEQ_TILE
    assert seq % tile == 0 and D_FF % FF_CHUNK == 0
    const = lambda shape: pl.BlockSpec(shape, lambda b, t: (0,) * len(shape),
                                       pipeline_mode=pl.Buffered(1))
    resident = (2 * 2 * tile * D_MODEL * 4 + 3 * D_MODEL * D_FF * 2 + tile * D_FF * 2
                + 4 * tile * FF_CHUNK * 4)
    return pl.pallas_call(
        _ffn_kernel,
        grid=(bsz, seq // tile),
        in_specs=[
            pl.BlockSpec((1, tile, D_MODEL), lambda b, t: (b, t, 0)),
            pl.BlockSpec((1, N_MOD, D_MODEL), lambda b, t: (b, 0, 0)),
            const((1, D_MODEL)),
            const((D_MODEL, 2 * D_FF)),
            const((3, D_FF)),
            const((1, D_FF)),
            const((D_FF, D_MODEL)),
            const((1, D_MODEL)),
        ],
        out_specs=pl.BlockSpec((1, tile, D_MODEL), lambda b, t: (b, t, 0)),
        out_shape=jax.ShapeDtypeStruct(x.shape, F32),
        scratch_shapes=[
            pltpu.VMEM((tile, D_FF), BF16),
            pltpu.VMEM((V7X_SUBLANES, D_FF), F32),
        ],
        compiler_params=pltpu.CompilerParams(
            dimension_semantics=("arbitrary", "arbitrary"),
            vmem_limit_bytes=_vmem_limit(resident)),
        name="channel_mixer",
    )(x, mod, norm2_g.reshape(1, -1), w_up, conv_w, conv_b.reshape(1, -1), w_down,
      final_g.reshape(1, -1))


def kernel(x, c, positions, ada_w, ada_b, norm1_g, w_in, b_in, conv_w, attn_sinks, out_norm_attn_g, out_norm_conv_g, w_o, norm2_g, w_up, ffn_conv_w, ffn_conv_b, w_down, final_norm_g):
    depth = ada_w.shape[0]
    bsz = x.shape[0]
    for l in range(depth):
        mod = _modulation(c, ada_w[l], ada_b[l]).reshape(bsz, N_MOD, D_MODEL)
        x = _token_mixer(x, positions, mod, norm1_g[l], w_in[l].astype(BF16), b_in[l], conv_w[l],
                         attn_sinks[l], out_norm_attn_g[l], out_norm_conv_g[l], w_o[l].astype(BF16))
        last = l == depth - 1
        assert last, "the final RMSNorm is fused into the last layer's channel mixer"
        x = _channel_mixer(x, mod, norm2_g[l], w_up[l].astype(BF16), ffn_conv_w[l], ffn_conv_b[l],
                           w_down[l].astype(BF16), final_norm_g)
    return x
```

```python
import math

import jax
import jax.numpy as jnp
from jax import lax
from jax.experimental import pallas as pl
from jax.experimental.pallas import tpu as pltpu

D_MODEL = 1024
HEAD_DIM = 64
N_Q_HEADS = 8
N_KV_HEADS = 2
D_ATTN = N_Q_HEADS * HEAD_DIM
D_KV = N_KV_HEADS * HEAD_DIM
D_CONV = D_MODEL - D_ATTN
D_QKV = D_ATTN + 2 * D_KV
D_IN = D_QKV + 3 * D_CONV
WINDOW = 128
D_FF = 2816
N_MOD = 6
ROPE_THETA = 10000.0
EPS = 1e-6
MASK_VALUE = -1e30
LOG2_E = math.log2(math.e)

V7X_SUBLANES = 8
V7X_MXU_COLS = 256
V7X_VMEM_BYTES = 64 * 1024 * 1024
V7X_VMEM_RESERVE = 4 * 1024 * 1024

SEQ_TILE = 1024
FF_CHUNK = V7X_MXU_COLS
PAIR = 2 * HEAD_DIM
HEAD_ROWS = 128
TAIL_ROW_BLOCKS = 4

F32 = jnp.float32
BF16 = jnp.bfloat16


def _vmem_limit(resident_bytes):
    return min(2 * resident_bytes, V7X_VMEM_BYTES - V7X_VMEM_RESERVE)


def _rms_norm(x, g):
    ms = jnp.mean(x * x, axis=-1, keepdims=True)
    return x * lax.rsqrt(ms + EPS) * g


def _modulated_norm(x, scale, shift):
    ms = jnp.mean(x * x, axis=-1, keepdims=True)
    return (x * lax.rsqrt(ms + EPS)).astype(BF16) * scale.astype(BF16) + shift.astype(BF16)


def _causal_conv3(cur, prev, w):
    rows = lax.broadcasted_iota(jnp.int32, (V7X_SUBLANES, cur.shape[1]), 0)
    p1 = prev[V7X_SUBLANES - 1:V7X_SUBLANES]
    p2 = prev[V7X_SUBLANES - 2:V7X_SUBLANES - 1]
    r1 = pltpu.roll(cur, 1, 0)
    r2 = pltpu.roll(cur, 2, 0)
    top1 = jnp.where(rows == 0, p1, r1[0:V7X_SUBLANES])
    top2 = jnp.where(rows == 0, p2, jnp.where(rows == 1, p1, r2[0:V7X_SUBLANES]))
    r1 = jnp.concatenate([top1, r1[V7X_SUBLANES:]], axis=0)
    r2 = jnp.concatenate([top2, r2[V7X_SUBLANES:]], axis=0)
    return r2 * w[0:1] + r1 * w[1:2] + cur * w[2:3]


def _mod_kernel(c_ref, w_ref, b_ref, o_ref):
    c_act = jax.nn.silu(c_ref[...])
    o_ref[...] = jnp.dot(c_act.astype(BF16), w_ref[...].astype(BF16),
                         preferred_element_type=F32) + b_ref[...]


def _modulation(c, ada_w, ada_b):
    bsz = c.shape[0]
    n = ada_w.shape[1]
    blk = 2 * D_MODEL
    return pl.pallas_call(
        _mod_kernel,
        grid=(n // blk,),
        in_specs=[
            pl.BlockSpec((bsz, D_MODEL), lambda j: (0, 0)),
            pl.BlockSpec((D_MODEL, blk), lambda j: (0, j)),
            pl.BlockSpec((1, blk), lambda j: (0, j)),
        ],
        out_specs=pl.BlockSpec((bsz, blk), lambda j: (0, j)),
        out_shape=jax.ShapeDtypeStruct((bsz, n), F32),
        name="adaln_modulation",
    )(c, ada_w, ada_b.reshape(1, n))


def _mixer_kernel(sinks_ref, x_ref, pos_ref, mod_ref, invf_ref, n1g_ref, win_ref, bin_ref,
                  cw_ref, ga_ref, gc_ref, wo_ref, o_ref,
                  q_s, k_s, v_s, ya_s, cc_s, s_s, m_s, p_s, t_s, y_s):
    t = pl.program_id(1)
    tile = x_ref.shape[1]
    nblk = tile // WINDOW

    lane = lax.broadcasted_iota(jnp.int32, (1, PAIR), 1)
    low_head = lane < HEAD_DIM
    first_half = (lane % HEAD_DIM) < (HEAD_DIM // 2)

    @pl.when(t > 0)
    def _():
        k_s[:, 0:WINDOW, :] = k_s[:, tile:tile + WINDOW, :]
        v_s[:, 0:WINDOW, 0:PAIR] = v_s[:, tile:tile + WINDOW, 0:PAIR]

    @pl.when(t == 0)
    def _():
        k_s[:, 0:WINDOW, :] = jnp.zeros((N_KV_HEADS, WINDOW, PAIR), BF16)
        v_s[:, 0:WINDOW, 0:PAIR] = jnp.zeros((2 * N_KV_HEADS, WINDOW, PAIR), BF16)
        ones_low = jnp.broadcast_to(jnp.where(low_head, 1.0, 0.0).astype(BF16), (tile + WINDOW, PAIR))
        ones_high = jnp.broadcast_to(jnp.where(low_head, 0.0, 1.0).astype(BF16), (tile + WINDOW, PAIR))
        for g in range(N_KV_HEADS):
            v_s[2 * g, :, PAIR:] = ones_low
            v_s[2 * g + 1, :, PAIR:] = ones_high
        cc_s[...] = jnp.zeros(cc_s.shape, F32)

    x = x_ref[0]
    mod = mod_ref[0]
    h = _rms_norm(x, n1g_ref[...] * (1.0 + mod[1:2])) + mod[0:1]
    hb = h.astype(BF16)

    qkv = jnp.dot(hb, win_ref[:, 0:D_QKV], preferred_element_type=F32) + bin_ref[:, 0:D_QKV]

    ang = invf_ref[...] * pos_ref[0].astype(F32)
    cos_f, sin_f = jnp.cos(ang), jnp.sin(ang)
    cos_t = jnp.concatenate([cos_f, cos_f, cos_f, cos_f], axis=0).T
    sin_t = jnp.concatenate([-sin_f, sin_f, -sin_f, sin_f], axis=0).T

    def rope(u):
        rot = jnp.where(first_half, pltpu.roll(u, PAIR - HEAD_DIM // 2, 1),
                        pltpu.roll(u, HEAD_DIM // 2, 1))
        return u * cos_t + rot * sin_t

    scale = LOG2_E / math.sqrt(HEAD_DIM)
    for p in range(D_ATTN // PAIR):
        q_s[:, p * PAIR:(p + 1) * PAIR] = (rope(qkv[:, p * PAIR:(p + 1) * PAIR]) * scale).astype(BF16)

    k = rope(qkv[:, D_ATTN:D_ATTN + D_KV])
    k_sw = pltpu.roll(k, HEAD_DIM, 1)
    v = qkv[:, D_ATTN + D_KV:D_QKV]
    v_sw = pltpu.roll(v, HEAD_DIM, 1)
    cur = pl.ds(WINDOW, tile)
    k_s[0, cur, :] = jnp.where(low_head, k, k_sw).astype(BF16)
    k_s[1, cur, :] = jnp.where(low_head, k_sw, k).astype(BF16)
    v_s[0, cur, 0:PAIR] = jnp.where(low_head, v, 0.0).astype(BF16)
    v_s[1, cur, 0:PAIR] = jnp.where(low_head, 0.0, v_sw).astype(BF16)
    v_s[2, cur, 0:PAIR] = jnp.where(low_head, v_sw, 0.0).astype(BF16)
    v_s[3, cur, 0:PAIR] = jnp.where(low_head, 0.0, v).astype(BF16)

    cproj = jnp.dot(hb, win_ref[:, D_QKV:D_IN], preferred_element_type=F32) + bin_ref[:, D_QKV:D_IN]
    bx = cproj[:, 0:D_CONV] * cproj[:, 2 * D_CONV:3 * D_CONV]
    y_conv = cproj[:, D_CONV:2 * D_CONV] * _causal_conv3(bx, cc_s[...], cw_ref[...])
    cc_s[...] = bx[tile - V7X_SUBLANES:tile]
    y_s[:, D_ATTN:] = _rms_norm(y_conv, gc_ref[...]).astype(BF16)

    kj = lax.broadcasted_iota(jnp.int32, (2 * WINDOW, WINDOW), 0)
    qi = lax.broadcasted_iota(jnp.int32, (2 * WINDOW, WINDOW), 1)
    diff = qi + WINDOW - kj
    band = (diff >= 0) & (diff < WINDOW)
    bias = jnp.where(band, 0.0, MASK_VALUE).astype(BF16)
    bias_first = jnp.where(band & ((kj >= WINDOW) | (t > 0)), 0.0, MASK_VALUE).astype(BF16)
    one_hot = (lax.broadcasted_iota(jnp.int32, (N_Q_HEADS // N_KV_HEADS * WINDOW, WINDOW), 0) % WINDOW
               == lax.broadcasted_iota(jnp.int32, (N_Q_HEADS // N_KV_HEADS * WINDOW, WINDOW), 1)).astype(BF16)
    group = N_Q_HEADS // N_KV_HEADS

    units = [(i, g) for i in range(nblk) for g in range(N_KV_HEADS)]

    for u, (i, g) in enumerate(units):
        pieces = []
        for p in range(g * group // 2, (g + 1) * group // 2):
            qp = q_s[pl.ds(i * WINDOW, WINDOW), p * PAIR:(p + 1) * PAIR]
            zero = jnp.zeros_like(qp)
            pieces += [jnp.where(low_head, qp, zero), jnp.where(low_head, zero, qp)]
        lhs = jnp.concatenate([jnp.concatenate(pieces, axis=0), one_hot], axis=1)
        rhs = jnp.concatenate([k_s[g, pl.ds(i * WINDOW, 2 * WINDOW), :],
                               bias_first if i == 0 else bias], axis=1)
        s = lax.dot_general(lhs, rhs, (((1,), (1,)), ((), ())), preferred_element_type=F32)
        for j in range(group):
            sj = s[j * WINDOW:(j + 1) * WINDOW]
            s_s[u * group + j] = sj
            m = jnp.maximum(jnp.max(sj, axis=-1, keepdims=True), sinks_ref[g * group + j] * LOG2_E)
            m_s[u * group + j] = jnp.broadcast_to(m, (WINDOW, PAIR))

    def exponentials(u, i, g):
        for j in range(group):
            m = m_s[u * group + j]
            e = jnp.exp2(s_s[u * group + j] - jnp.concatenate([m, m], axis=1))
            p_s[u, pl.ds((j // 2) * WINDOW, WINDOW), pl.ds((j % 2) * 2 * WINDOW, 2 * WINDOW)] = e.astype(BF16)
        for jj in range(group // 2):
            sink_lo = sinks_ref[g * group + 2 * jj] * LOG2_E - m_s[u * group + 2 * jj]
            sink_hi = sinks_ref[g * group + 2 * jj + 1] * LOG2_E - m_s[u * group + 2 * jj + 1]
            t_s[u * (group // 2) + jj] = jnp.exp2(jnp.where(low_head, sink_lo, sink_hi))

    def weighted_values(u, i, g):
        krows = pl.ds(i * WINDOW, 2 * WINDOW)
        rhs = jnp.concatenate([v_s[2 * g, krows, :], v_s[2 * g + 1, krows, :]], axis=0)
        o = jnp.dot(p_s[u], rhs, preferred_element_type=F32)
        for jj in range(group // 2):
            oj = o[jj * WINDOW:(jj + 1) * WINDOW]
            den = oj[:, PAIR:] + t_s[u * (group // 2) + jj]
            p = g * group // 2 + jj
            ya_s[pl.ds(i * WINDOW, WINDOW), p * PAIR:(p + 1) * PAIR] = oj[:, 0:PAIR] / den

    for u, (i, g) in enumerate(units):
        exponentials(u, i, g)
    for u, (i, g) in enumerate(units):
        weighted_values(u, i, g)

    y_s[:, 0:D_ATTN] = _rms_norm(ya_s[...], ga_ref[...]).astype(BF16)
    o_ref[0] = x_ref[0] + mod[2:3] * jnp.dot(y_s[...], wo_ref[...], preferred_element_type=F32)


def _token_mixer(x, positions, mod, norm1_g, w_in, b_in, conv_w, sinks, ga, gc, w_o):
    bsz, seq, _ = x.shape
    tile = SEQ_TILE
    assert seq % tile == 0 and tile % WINDOW == 0
    inv_freq = ROPE_THETA ** (-jnp.arange(0, HEAD_DIM // 2, dtype=F32) / (HEAD_DIM // 2))
    const = lambda shape: pl.BlockSpec(shape, lambda b, t: (0,) * len(shape),
                                       pipeline_mode=pl.Buffered(1))
    nhb = (tile // WINDOW) * N_Q_HEADS
    resident = (2 * 2 * tile * D_MODEL * 4 + (D_MODEL * D_IN + D_MODEL * D_MODEL) * 2
                + tile * D_IN * 4 + tile * (D_ATTN * 6 + 6 * PAIR * 2)
                + nhb * WINDOW * WINDOW * (8 + 4 + 2 + 2))
    return pl.pallas_call(
        _mixer_kernel,
        grid=(bsz, seq // tile),
        in_specs=[
            pl.BlockSpec(memory_space=pltpu.SMEM),
            pl.BlockSpec((1, tile, D_MODEL), lambda b, t: (b, t, 0)),
            pl.BlockSpec((1, 1, tile), lambda b, t: (b, 0, t)),
            pl.BlockSpec((1, N_MOD, D_MODEL), lambda b, t: (b, 0, 0)),
            const((HEAD_DIM // 2, 1)),
            const((1, D_MODEL)),
            const((D_MODEL, D_IN)),
            const((1, D_IN)),
            const((3, D_CONV)),
            const((1, D_ATTN)),
            const((1, D_CONV)),
            const((D_MODEL, D_MODEL)),
        ],
        out_specs=pl.BlockSpec((1, tile, D_MODEL), lambda b, t: (b, t, 0)),
        out_shape=jax.ShapeDtypeStruct(x.shape, F32),
        scratch_shapes=[
            pltpu.VMEM((tile, D_ATTN), BF16),
            pltpu.VMEM((N_KV_HEADS, tile + WINDOW, PAIR), BF16),
            pltpu.VMEM((2 * N_KV_HEADS, tile + WINDOW, 2 * PAIR), BF16),
            pltpu.VMEM((tile, D_ATTN), F32),
            pltpu.VMEM((V7X_SUBLANES, D_CONV), F32),
            pltpu.VMEM((nhb, WINDOW, 2 * WINDOW), F32),
            pltpu.VMEM((nhb, WINDOW, PAIR), F32),
            pltpu.VMEM((nhb // 4, 2 * WINDOW, 4 * WINDOW), BF16),
            pltpu.VMEM((nhb // 2, WINDOW, PAIR), F32),
            pltpu.VMEM((tile, D_MODEL), BF16),
        ],
        compiler_params=pltpu.CompilerParams(
            dimension_semantics=("arbitrary", "arbitrary"),
            vmem_limit_bytes=_vmem_limit(resident)),
        name="token_mixer",
    )(sinks, x, positions.reshape(bsz, 1, seq), mod, inv_freq.reshape(-1, 1),
      norm1_g.reshape(1, -1), w_in, b_in.reshape(1, -1), conv_w, ga.reshape(1, -1),
      gc.reshape(1, -1), w_o)


def _ffn_kernel(x_ref, mod_ref, n2g_ref, wup_ref, cw_ref, cb_ref, wdn_ref, fg_ref, o_ref,
                f_s, gc_s):
    t = pl.program_id(1)
    tile = x_ref.shape[1]

    @pl.when(t == 0)
    def _():
        gc_s[...] = jnp.zeros(gc_s.shape, F32)

    x = x_ref[0]
    mod = mod_ref[0]
    hb = _modulated_norm(x, n2g_ref[...] * (1.0 + mod[4:5]), mod[3:4])

    for c in range(D_FF // FF_CHUNK):
        cols = pl.ds(c * FF_CHUNK, FF_CHUNK)
        vcols = pl.ds(D_FF + c * FF_CHUNK, FF_CHUNK)
        if c == 0:
            blocks = [hb[0:HEAD_ROWS], hb[HEAD_ROWS:]]
            gate = jnp.concatenate([jnp.dot(b, wup_ref[:, cols], preferred_element_type=F32)
                                    for b in blocks], axis=0)
            val = jnp.concatenate([jnp.dot(b, wup_ref[:, vcols], preferred_element_type=F32)
                                   for b in blocks], axis=0)
        else:
            gate = jnp.dot(hb, wup_ref[:, cols], preferred_element_type=F32)
            val = jnp.dot(hb, wup_ref[:, vcols], preferred_element_type=F32)
        conv = _causal_conv3(gate, gc_s[:, cols], cw_ref[:, cols]) + cb_ref[:, cols]
        gc_s[:, cols] = gate[tile - V7X_SUBLANES:tile]
        f_s[:, cols] = (jax.nn.silu(conv) * val).astype(BF16)

    rb = tile // TAIL_ROW_BLOCKS
    for r in range(TAIL_ROW_BLOCKS):
        rows = pl.ds(r * rb, rb)
        x2 = x_ref[0, rows, :] + mod[5:6] * jnp.dot(f_s[rows, :], wdn_ref[...], preferred_element_type=F32)
        o_ref[0, rows, :] = _rms_norm(x2, fg_ref[...])


def _channel_mixer(x, mod, norm2_g, w_up, conv_w, conv_b, w_down, final_g):
    bsz, seq, _ = x.shape
    tile = SEQ_TILE
    assert seq % tile == 0 and D_FF % FF_CHUNK == 0
    const = lambda shape: pl.BlockSpec(shape, lambda b, t: (0,) * len(shape),
                                       pipeline_mode=pl.Buffered(1))
    resident = (2 * 2 * tile * D_MODEL * 4 + 3 * D_MODEL * D_FF * 2 + tile * D_FF * 2
                + 4 * tile * FF_CHUNK * 4)
    return pl.pallas_call(
        _ffn_kernel,
        grid=(bsz, seq // tile),
        in_specs=[
            pl.BlockSpec((1, tile, D_MODEL), lambda b, t: (b, t, 0)),
            pl.BlockSpec((1, N_MOD, D_MODEL), lambda b, t: (b, 0, 0)),
            const((1, D_MODEL)),
            const((D_MODEL, 2 * D_FF)),
            const((3, D_FF)),
            const((1, D_FF)),
            const((D_FF, D_MODEL)),
            const((1, D_MODEL)),
        ],
        out_specs=pl.BlockSpec((1, tile, D_MODEL), lambda b, t: (b, t, 0)),
        out_shape=jax.ShapeDtypeStruct(x.shape, F32),
        scratch_shapes=[
            pltpu.VMEM((tile, D_FF), BF16),
            pltpu.VMEM((V7X_SUBLANES, D_FF), F32),
        ],
        compiler_params=pltpu.CompilerParams(
            dimension_semantics=("arbitrary", "arbitrary"),
            vmem_limit_bytes=_vmem_limit(resident)),
        name="channel_mixer",
    )(x, mod, norm2_g.reshape(1, -1), w_up, conv_w, conv_b.reshape(1, -1), w_down,
      final_g.reshape(1, -1))


def kernel(x, c, positions, ada_w, ada_b, norm1_g, w_in, b_in, conv_w, attn_sinks, out_norm_attn_g, out_norm_conv_g, w_o, norm2_g, w_up, ffn_conv_w, ffn_conv_b, w_down, final_norm_g):
    depth = ada_w.shape[0]
    bsz = x.shape[0]
    for l in range(depth):
        mod = _modulation(c, ada_w[l], ada_b[l]).reshape(bsz, N_MOD, D_MODEL)
        x = _token_mixer(x, positions, mod, norm1_g[l], w_in[l].astype(BF16), b_in[l], conv_w[l],
                         attn_sinks[l], out_norm_attn_g[l], out_norm_conv_g[l], w_o[l].astype(BF16))
        last = l == depth - 1
        assert last, "the final RMSNorm is fused into the last layer's channel mixer"
        x = _channel_mixer(x, mod, norm2_g[l], w_up[l].astype(BF16), ffn_conv_w[l], ffn_conv_b[l],
                           w_down[l].astype(BF16), final_norm_g)
    return x
```
